```python
import jax
import jax.numpy as jnp
from jax import lax
import numpy as np

D_MODEL = 4096
BATCH = 2
SEQ = 4096
DEPTH = 2

GRID_W = 64
CTX_LEN = 256
MIX_WIDTH = D_MODEL
POOL_WIDTH = MIX_WIDTH // 4
POOL_WINDOWS = (2, 4, 8, 16)
POOL_GROUPS = len(POOL_WINDOWS)
POOL_GROUP_DIM = POOL_WIDTH // POOL_GROUPS
CONV_WIDTH = MIX_WIDTH // 4
CONV_KSIZE = 31
ATTN_WIDTH = MIX_WIDTH - POOL_WIDTH - CONV_WIDTH
HEAD_DIM = 128
N_HEADS = ATTN_WIDTH // HEAD_DIM
WIN_H = 8
WIN_W = 16
D_FF = 4 * D_MODEL
N_MOD = 6
EPS = 1e-6

OFF_CONV = POOL_WIDTH
OFF_Q = OFF_CONV + 2 * CONV_WIDTH
OFF_K = OFF_Q + ATTN_WIDTH
OFF_V = OFF_K + ATTN_WIDTH
IN_WIDTH = OFF_V + ATTN_WIDTH

kernel_name = 'hybrid_dit_pool_conv_natten_block'


def rms_norm(x, g):
    xf = x.astype(jnp.float32)
    y = xf * lax.rsqrt(jnp.mean(xf * xf, axis=-1, keepdims=True) + EPS)
    return (y * g.astype(jnp.float32)).astype(x.dtype)


def modulate(x, g, shift, scale):
    return rms_norm(x, g) * (1 + scale) + shift


def adaln_params(cvec, w_ada, b_ada):
    m = jax.nn.silu(cvec) @ w_ada + b_ada
    return m.reshape(cvec.shape[0], N_MOD, 1, D_MODEL)


def split_heads(u):
    return u.reshape(*u.shape[:-1], N_HEADS, HEAD_DIM)


def pool_mixer(u, pool_w, pool_scale):
    b, l, _ = u.shape
    ug = u.reshape(b, l, POOL_GROUPS, POOL_GROUP_DIM)
    cs = jnp.cumsum(ug.astype(jnp.float32), axis=1)
    cs = jnp.pad(cs, ((0, 0), (1, 0), (0, 0), (0, 0)))
    t = jnp.arange(l)[:, None]
    win = jnp.array(POOL_WINDOWS, dtype=jnp.int32)[None, :]
    lo = jnp.clip(t - win // 2, 0, l - 1)
    hi = jnp.clip(t + (win - 1 - win // 2), 0, l - 1)
    g_idx = jnp.arange(POOL_GROUPS)[None, :]
    window_sum = cs[:, hi + 1, g_idx] - cs[:, lo, g_idx]
    count = (hi - lo + 1).astype(jnp.float32)[None, :, :, None]
    pooled = (window_sum / count - ug.astype(jnp.float32)).astype(u.dtype)
    mixed = jnp.einsum('blgc,gcd->blgd', pooled, pool_w)
    return mixed.reshape(b, l, POOL_WIDTH) * pool_scale


def conv_mixer(u, dw_w, dw_b, norm_g, pw_w):
    a, gate = jnp.split(u, 2, axis=-1)
    h = a * jax.nn.sigmoid(gate)
    h = lax.conv_general_dilated(
        h, dw_w[:, None, :], window_strides=(1,),
        padding=((CONV_KSIZE // 2, CONV_KSIZE // 2),),
        dimension_numbers=('NWC', 'WIO', 'NWC'),
        feature_group_count=CONV_WIDTH) + dw_b
    h = jax.nn.silu(rms_norm(h, norm_g))
    return h @ pw_w


def neighbourhood_attention(q, k, v, kc, vc, rpb):
    b, l, h, dh = q.shape
    rows = l // GRID_W
    kh = min(WIN_H, rows)
    scale = dh ** -0.5
    i = jnp.arange(rows)
    row_start = jnp.clip(i - kh // 2, 0, rows - kh)
    dr = row_start[:, None] + jnp.arange(kh)[None, :] - i[:, None] + (WIN_H - 1)
    j = jnp.arange(GRID_W)
    col_start = jnp.clip(j - WIN_W // 2, 0, GRID_W - WIN_W)
    col_valid = (j[None, :] >= col_start[:, None]) & (j[None, :] < col_start[:, None] + WIN_W)
    dc = jnp.clip(j[None, :] - j[:, None] + (WIN_W - 1), 0, 2 * WIN_W - 2)
    k_grid = k.reshape(b, rows, GRID_W, h, dh)
    v_grid = v.reshape(b, rows, GRID_W, h, dh)
    q_rows = jnp.moveaxis(q.reshape(b, rows, GRID_W, h, dh), 1, 0)
    kc32 = kc.astype(jnp.float32)

    def row_block(args):
        q_row, start, dr_row = args
        k_blk = lax.dynamic_slice_in_dim(k_grid, start, kh, axis=1)
        v_blk = lax.dynamic_slice_in_dim(v_grid, start, kh, axis=1)
        bias = rpb[:, dr_row[None, :, None], dc[:, None, :]].astype(jnp.float32)
        bias = jnp.where(col_valid[None, :, None, :], bias, -jnp.inf)
        q32 = q_row.astype(jnp.float32)
        s_nb = jnp.einsum('bjhd,brchd->bhjrc', q32, k_blk.astype(jnp.float32)) * scale + bias[None]
        s_ctx = jnp.einsum('bjhd,bnhd->bhjn', q32, kc32) * scale
        s = jnp.concatenate([s_nb.reshape(b, h, GRID_W, kh * GRID_W), s_ctx], axis=-1)
        p = jax.nn.softmax(s, axis=-1).astype(v.dtype)
        p_nb = p[..., :kh * GRID_W].reshape(b, h, GRID_W, kh, GRID_W)
        p_ctx = p[..., kh * GRID_W:]
        return (jnp.einsum('bhjrc,brchd->bjhd', p_nb, v_blk)
                + jnp.einsum('bhjn,bnhd->bjhd', p_ctx, vc))

    o = lax.map(row_block, (q_rows, row_start, dr))
    return jnp.moveaxis(o, 0, 1).reshape(b, l, h * dh)


def context_attention(qc, kc, vc):
    b, n, h, dh = qc.shape
    s = jnp.einsum('bqhd,bkhd->bhqk', qc.astype(jnp.float32), kc.astype(jnp.float32)) * dh ** -0.5
    p = jax.nn.softmax(s, axis=-1).astype(vc.dtype)
    return jnp.einsum('bhqk,bkhd->bqhd', p, vc).reshape(b, n, h * dh)


def sq_relu_mlp(h, w1, w2):
    return jnp.square(jax.nn.relu(h @ w1)) @ w2


def group_outputs(u, y_attn, pool_w, pool_scale, conv_dw_w, conv_dw_b, conv_norm_g, conv_pw_w):
    y_pool = pool_mixer(u[..., :OFF_CONV], pool_w, pool_scale)
    y_conv = conv_mixer(u[..., OFF_CONV:OFF_Q], conv_dw_w, conv_dw_b, conv_norm_g, conv_pw_w)
    return jnp.concatenate([y_pool, y_conv, y_attn], axis=-1)


def hybrid_layer(x, xc, mod, mod_c, norm1_g, norm2_g, w_in, pool_w, pool_scale,
                 conv_dw_w, conv_dw_b, conv_norm_g, conv_pw_w, q_norm_g, k_norm_g,
                 rpb, w_out, w_mlp1, w_mlp2, update_ctx):
    hc = modulate(xc, norm1_g, mod_c[:, 0], mod_c[:, 1])
    col0 = 0 if update_ctx else OFF_K
    uc = hc @ w_in[:, col0:]
    kc = rms_norm(split_heads(uc[..., OFF_K - col0:OFF_V - col0]), k_norm_g)
    vc = split_heads(uc[..., OFF_V - col0:])

    h = modulate(x, norm1_g, mod[:, 0], mod[:, 1])
    u = h @ w_in
    q = rms_norm(split_heads(u[..., OFF_Q:OFF_K]), q_norm_g)
    k = rms_norm(split_heads(u[..., OFF_K:OFF_V]), k_norm_g)
    v = split_heads(u[..., OFF_V:])
    y_attn = neighbourhood_attention(q, k, v, kc, vc, rpb)
    y = group_outputs(u, y_attn, pool_w, pool_scale, conv_dw_w, conv_dw_b, conv_norm_g, conv_pw_w)
    x = x + mod[:, 2] * (y @ w_out)
    x = x + mod[:, 5] * sq_relu_mlp(modulate(x, norm2_g, mod[:, 3], mod[:, 4]), w_mlp1, w_mlp2)

    if update_ctx:
        qc = rms_norm(split_heads(uc[..., OFF_Q:OFF_K]), q_norm_g)
        yc_attn = context_attention(qc, kc, vc)
        yc = group_outputs(uc, yc_attn, pool_w, pool_scale, conv_dw_w, conv_dw_b, conv_norm_g, conv_pw_w)
        xc = xc + mod_c[:, 2] * (yc @ w_out)
        xc = xc + mod_c[:, 5] * sq_relu_mlp(modulate(xc, norm2_g, mod_c[:, 3], mod_c[:, 4]), w_mlp1, w_mlp2)
    return x, xc


def setup_inputs(seed: int = 0) -> dict:
    key = jax.random.key(seed)
    ks = jax.random.split(key, 22)

    def nrm(k, shape, s):
        return jax.random.normal(k, shape, jnp.float32) * s

    return {
        'x': nrm(ks[0], (BATCH, SEQ, D_MODEL), 1.0),
        'c': nrm(ks[1], (BATCH, D_MODEL), 1.0),
        'ctx': nrm(ks[2], (BATCH, CTX_LEN, D_MODEL), 1.0),
        'c_ctx': nrm(ks[3], (D_MODEL,), 1.0),
        'w_ada': nrm(ks[4], (DEPTH, D_MODEL, N_MOD * D_MODEL), 0.5 * D_MODEL ** -0.5),
        'b_ada': nrm(ks[5], (DEPTH, N_MOD * D_MODEL), 0.02),
        'norm1_g': 1.0 + nrm(ks[6], (DEPTH, D_MODEL), 0.1),
        'norm2_g': 1.0 + nrm(ks[7], (DEPTH, D_MODEL), 0.1),
        'w_in': nrm(ks[8], (DEPTH, D_MODEL, IN_WIDTH), D_MODEL ** -0.5),
        'pool_w': nrm(ks[9], (DEPTH, POOL_GROUPS, POOL_GROUP_DIM, POOL_GROUP_DIM), POOL_GROUP_DIM ** -0.5),
        'pool_scale': 1.0 + nrm(ks[10], (DEPTH, POOL_WIDTH), 0.1),
        'conv_dw_w': nrm(ks[11], (DEPTH, CONV_KSIZE, CONV_WIDTH), CONV_KSIZE ** -0.5),
        'conv_dw_b': nrm(ks[12], (DEPTH, CONV_WIDTH), 0.02),
        'conv_norm_g': 1.0 + nrm(ks[13], (DEPTH, CONV_WIDTH), 0.1),
        'conv_pw_w': nrm(ks[14], (DEPTH, CONV_WIDTH, CONV_WIDTH), CONV_WIDTH ** -0.5),
        'q_norm_g': 1.0 + nrm(ks[15], (DEPTH, HEAD_DIM), 0.1),
        'k_norm_g': 1.0 + nrm(ks[16], (DEPTH, HEAD_DIM), 0.1),
        'rpb': nrm(ks[17], (DEPTH, N_HEADS, 2 * WIN_H - 1, 2 * WIN_W - 1), 0.5),
        'w_out': nrm(ks[18], (DEPTH, MIX_WIDTH, D_MODEL), MIX_WIDTH ** -0.5),
        'w_mlp1': nrm(ks[19], (DEPTH, D_MODEL, D_FF), D_MODEL ** -0.5),
        'w_mlp2': nrm(ks[20], (DEPTH, D_FF, D_MODEL), D_FF ** -0.5),
    }


def reference(x, c, ctx, c_ctx, w_ada, b_ada, norm1_g, norm2_g, w_in, pool_w, pool_scale,
              conv_dw_w, conv_dw_b, conv_norm_g, conv_pw_w, q_norm_g, k_norm_g, rpb,
              w_out, w_mlp1, w_mlp2):
    xc = ctx
    for layer in range(DEPTH):
        mod = adaln_params(c, w_ada[layer], b_ada[layer])
        mod_c = adaln_params(c_ctx[None, :], w_ada[layer], b_ada[layer])
        x, xc = hybrid_layer(
            x, xc, mod, mod_c, norm1_g[layer], norm2_g[layer], w_in[layer],
            pool_w[layer], pool_scale[layer], conv_dw_w[layer], conv_dw_b[layer],
            conv_norm_g[layer], conv_pw_w[layer], q_norm_g[layer], k_norm_g[layer],
            rpb[layer], w_out[layer], w_mlp1[layer], w_mlp2[layer],
            update_ctx=layer < DEPTH - 1)
    return x
```

```python
import functools
from typing import NamedTuple

import jax
import jax.numpy as jnp
from jax import lax
from jax.experimental import pallas as pl
from jax.experimental.pallas import tpu as pltpu

GRID_W = 64
WIN_H = 8
WIN_W = 16
HEAD_DIM = 128
POOL_WINDOWS = (2, 4, 8, 16)
CONV_KSIZE = 31
N_MOD = 6
EPS = 1e-6
MASK_VALUE = -1e30

V7X_LANES = 128
V7X_SUBLANES = 8
V7X_VMEM_LIMIT_BYTES = 56 * 1024 * 1024
SEQ_PAD = 16

_BF16 = jnp.bfloat16
_F32 = jnp.float32


def _params(*sem):
    return pltpu.CompilerParams(dimension_semantics=sem, vmem_limit_bytes=V7X_VMEM_LIMIT_BYTES)


def _tile(n, pref, *also):
    if n <= pref and not any(a % n for a in also):
        return n
    t = min(pref, n) - min(pref, n) % V7X_LANES
    while n % t or any(a % t for a in also):
        t -= V7X_LANES
    return t


def _dot(a, b):
    return jnp.dot(a, b, preferred_element_type=_F32)


def _ada_kernel(c_ref, w_ref, b_ref, o_ref):
    c = c_ref[...]
    s = (c * jax.nn.sigmoid(c)).astype(_BF16)
    o_ref[...] = _dot(s, w_ref[...].astype(_BF16)) + b_ref[...]


def _ada(cvec, w_ada, b_ada):
    r, d = cvec.shape
    n = w_ada.shape[1]
    tn = _tile(n, 1024)
    return pl.pallas_call(
        _ada_kernel,
        out_shape=jax.ShapeDtypeStruct((r, n), _F32),
        grid=(n // tn,),
        in_specs=[
            pl.BlockSpec((r, d), lambda j: (0, 0)),
            pl.BlockSpec((d, tn), lambda j: (0, j)),
            pl.BlockSpec((1, tn), lambda j: (0, j)),
        ],
        out_specs=pl.BlockSpec((r, tn), lambda j: (0, j)),
        compiler_params=_params("parallel"),
        name="ada",
    )(cvec, w_ada, b_ada.reshape(1, n))


class _Rows(NamedTuple):
    base: int
    tokens: int

    def of_block(self, i, tm):
        return self.base + (i * tm) // self.tokens


def _mod_spec(tn, which, rows, tm):
    return pl.BlockSpec((None, None, 1, tn), lambda i, j=0, *_: (rows.of_block(i, tm), which, 0, j))


def _normmod_kernel(x_ref, g_ref, shift_ref, scale_ref, o_ref):
    x = x_ref[...]
    y = x * lax.rsqrt(jnp.mean(x * x, axis=-1, keepdims=True) + EPS) * g_ref[...]
    o_ref[...] = (y * (1.0 + scale_ref[...]) + shift_ref[...]).astype(o_ref.dtype)


def _normmod(x, g, mod, shift_idx, rows, tm):
    m, d = x.shape
    return pl.pallas_call(
        _normmod_kernel,
        out_shape=jax.ShapeDtypeStruct((m, d), _BF16),
        grid=(m // tm,),
        in_specs=[
            pl.BlockSpec((tm, d), lambda i: (i, 0)),
            pl.BlockSpec((1, d), lambda i: (0, 0)),
            _mod_spec(d, shift_idx, rows, tm),
            _mod_spec(d, shift_idx + 1, rows, tm),
        ],
        out_specs=pl.BlockSpec((tm, d), lambda i: (i, 0)),
        compiler_params=_params("parallel"),
        name="normmod",
    )(x, g.reshape(1, d), mod, mod)


def _proj_plain_kernel(h_ref, w_ref, o_ref):
    o_ref[...] = _dot(h_ref[...], w_ref[...]).astype(o_ref.dtype)


def _proj_kernel(h_ref, w_ref, g_ref, o_ref, *, n_norm_blocks):
    acc = _dot(h_ref[...], w_ref[...])
    j = pl.program_id(1)

    @pl.when(j < n_norm_blocks)
    def _():
        for h in range(acc.shape[1] // HEAD_DIM):
            sl = slice(h * HEAD_DIM, (h + 1) * HEAD_DIM)
            a = acc[:, sl]
            y = a * lax.rsqrt(jnp.mean(a * a, axis=-1, keepdims=True) + EPS) * g_ref[:, sl]
            o_ref[:, sl] = y.astype(o_ref.dtype)

    @pl.when(j >= n_norm_blocks)
    def _():
        o_ref[...] = acc.astype(o_ref.dtype)


def _proj(h, w, col0, ncols, gains, out_dtype, tm, tn, name):
    m, d = h.shape
    tn = _tile(ncols, tn, col0, *(() if gains is None else (gains.shape[1],)))
    cb = col0 // tn
    in_specs = [
        pl.BlockSpec((tm, d), lambda i, j: (i, 0)),
        pl.BlockSpec((d, tn), lambda i, j: (0, cb + j)),
    ]
    if gains is None:
        body, args = _proj_plain_kernel, (h, w)
    else:
        n_norm_blocks = gains.shape[1] // tn
        body, args = functools.partial(_proj_kernel, n_norm_blocks=n_norm_blocks), (h, w, gains)
        in_specs.append(pl.BlockSpec((1, tn), lambda i, j: (0, jnp.minimum(j, n_norm_blocks - 1))))
    return pl.pallas_call(
        body,
        out_shape=jax.ShapeDtypeStruct((m, ncols), out_dtype),
        grid=(m // tm, ncols // tn),
        in_specs=in_specs,
        out_specs=pl.BlockSpec((tm, tn), lambda i, j: (i, j)),
        compiler_params=_params("parallel", "parallel"),
        name=name,
    )(*args)


def _nattn_kernel(q_ref, k_ref, v_ref, kc_ref, vc_ref, b_ref, o_ref, *, rows, kh):
    scale = HEAD_DIM ** -0.5
    nt = (((1,), (1,)), ((), ()))
    kc = kc_ref[...]
    vc = vc_ref[...]

    def row(i, carry):
        start = jnp.clip(i - kh // 2, 0, rows - kh)
        q = q_ref[pl.ds(pl.multiple_of(i * GRID_W, GRID_W), GRID_W), :]
        win = pl.ds(pl.multiple_of(start * GRID_W, GRID_W), kh * GRID_W)
        s_nb = lax.dot_general(q, k_ref[win, :], nt, preferred_element_type=_F32) * scale
        s_nb = s_nb + b_ref[kh - 1 - (i - start)]
        s_cx = lax.dot_general(q, kc, nt, preferred_element_type=_F32) * scale
        m = jnp.maximum(jnp.max(s_nb, axis=-1, keepdims=True), jnp.max(s_cx, axis=-1, keepdims=True))
        e_nb = jnp.exp(s_nb - m)
        e_cx = jnp.exp(s_cx - m)
        inv = 1.0 / (jnp.sum(e_nb, axis=-1, keepdims=True) + jnp.sum(e_cx, axis=-1, keepdims=True))
        o = _dot((e_nb * inv).astype(_BF16), v_ref[win, :]) + _dot((e_cx * inv).astype(_BF16), vc)
        o_ref[pl.ds(pl.multiple_of(i * GRID_W, GRID_W), GRID_W), :] = o.astype(o_ref.dtype)
        return carry

    lax.fori_loop(0, rows, row, 0)


def _bias_table(rpb, kh):
    j = jnp.arange(GRID_W)
    col_start = jnp.clip(j - WIN_W // 2, 0, GRID_W - WIN_W)
    valid = (j[None, :] >= col_start[:, None]) & (j[None, :] < col_start[:, None] + WIN_W)
    dc = jnp.clip(j[None, :] - j[:, None] + (WIN_W - 1), 0, 2 * WIN_W - 2)
    t = jnp.where(valid[None, None], rpb[:, :, dc], MASK_VALUE)
    dr = (WIN_H - kh) + jnp.arange(kh)[:, None] + jnp.arange(kh)[None, :]
    full = t[:, dr]
    full = jnp.transpose(full, (0, 1, 3, 2, 4))
    return full.reshape(rpb.shape[0], kh, GRID_W, kh * GRID_W).astype(_F32)


def _nattn(qkv, kvc, kc_col, vc_col, bias, batch, seq, n_ctx, n_heads):
    rows = seq // GRID_W
    kh = min(WIN_H, rows)
    a = n_heads * HEAD_DIM
    hb = lambda off: off // HEAD_DIM
    return pl.pallas_call(
        functools.partial(_nattn_kernel, rows=rows, kh=kh),
        out_shape=jax.ShapeDtypeStruct((batch * seq, a), _BF16),
        grid=(batch, n_heads),
        in_specs=[
            pl.BlockSpec((seq, HEAD_DIM), lambda b, h: (b, h)),
            pl.BlockSpec((seq, HEAD_DIM), lambda b, h: (b, hb(a) + h)),
            pl.BlockSpec((seq, HEAD_DIM), lambda b, h: (b, hb(2 * a) + h)),
            pl.BlockSpec((n_ctx, HEAD_DIM), lambda b, h: (b, hb(kc_col) + h)),
            pl.BlockSpec((n_ctx, HEAD_DIM), lambda b, h: (b, hb(vc_col) + h)),
            pl.BlockSpec((None, kh, GRID_W, kh * GRID_W), lambda b, h: (h, 0, 0, 0)),
        ],
        out_specs=pl.BlockSpec((seq, HEAD_DIM), lambda b, h: (b, h)),
        compiler_params=_params("parallel", "parallel"),
        name="nattn",
    )(qkv, qkv, qkv, kvc, kvc, bias)


def _cattn_kernel(q_ref, k_ref, v_ref, o_ref):
    scale = HEAD_DIM ** -0.5
    s = lax.dot_general(q_ref[...], k_ref[...], (((1,), (1,)), ((), ())), preferred_element_type=_F32) * scale
    e = jnp.exp(s - jnp.max(s, axis=-1, keepdims=True))
    p = e * (1.0 / jnp.sum(e, axis=-1, keepdims=True))
    o_ref[...] = _dot(p.astype(_BF16), v_ref[...]).astype(o_ref.dtype)


def _cattn(qkv, batch, n_ctx, n_heads):
    a = n_heads * HEAD_DIM
    spec = lambda off: pl.BlockSpec((n_ctx, HEAD_DIM), lambda b, h: (b, off // HEAD_DIM + h))
    return pl.pallas_call(
        _cattn_kernel,
        out_shape=jax.ShapeDtypeStruct((batch * n_ctx, a), _BF16),
        grid=(batch, n_heads),
        in_specs=[spec(0), spec(a), spec(2 * a)],
        out_specs=spec(0),
        compiler_params=_params("parallel", "parallel"),
        name="cattn",
    )(qkv, qkv, qkv)


def _fill_padded(pad_ref, seq, fill_chunk, chunk):
    zeros = jnp.zeros((SEQ_PAD, pad_ref.shape[1]), pad_ref.dtype)
    pad_ref[pl.ds(0, SEQ_PAD), :] = zeros
    pad_ref[pl.ds(SEQ_PAD + seq, SEQ_PAD), :] = zeros

    def body(c, carry):
        t0 = pl.multiple_of(c * chunk, chunk)
        pad_ref[pl.ds(SEQ_PAD + t0, chunk), :] = fill_chunk(t0)
        return carry

    lax.fori_loop(0, seq // chunk, body, 0)


def _pool_kernel(u_ref, w_ref, s_ref, o_ref, pad_ref, *, seq, chunk):
    g = pl.program_id(1)
    _fill_padded(pad_ref, seq, lambda t0: u_ref[pl.ds(t0, chunk), :], chunk)
    w = w_ref[...]
    s = s_ref[...]

    for gi, win in enumerate(POOL_WINDOWS):

        @pl.when(g == gi)
        def _(win=win):
            def body(c, carry):
                t0 = pl.multiple_of(c * chunk, chunk)
                blk = pad_ref[pl.ds(t0, chunk + 2 * SEQ_PAD), :]
                tot = None
                for sft in range(-(win // 2), win - win // 2):
                    part = blk[SEQ_PAD + sft:SEQ_PAD + sft + chunk, :]
                    tot = part if tot is None else tot + part
                t = t0 + lax.broadcasted_iota(jnp.int32, (chunk, blk.shape[1]), 0)
                lo = jnp.maximum(t - win // 2, 0)
                hi = jnp.minimum(t + (win - 1 - win // 2), seq - 1)
                cnt = (hi - lo + 1).astype(_F32)
                pooled = tot / cnt - blk[SEQ_PAD:SEQ_PAD + chunk, :]
                o_ref[pl.ds(t0, chunk), :] = (_dot(pooled.astype(_BF16), w) * s).astype(o_ref.dtype)
                return carry

            lax.fori_loop(0, seq // chunk, body, 0)


def _pool(u, pool_w, pool_scale, batch, seq):
    n_groups, c, _ = pool_w.shape
    chunk = min(128, seq)
    return pl.pallas_call(
        functools.partial(_pool_kernel, seq=seq, chunk=chunk),
        out_shape=jax.ShapeDtypeStruct((batch * seq, n_groups * c), _BF16),
        grid=(batch, n_groups),
        in_specs=[
            pl.BlockSpec((seq, c), lambda b, g: (b, g)),
            pl.BlockSpec((None, c, c), lambda b, g: (g, 0, 0)),
            pl.BlockSpec((1, c), lambda b, g: (0, g)),
        ],
        out_specs=pl.BlockSpec((seq, c), lambda b, g: (b, g)),
        scratch_shapes=[pltpu.VMEM((seq + 2 * SEQ_PAD, c), _F32)],
        compiler_params=_params("parallel", "parallel"),
        name="pool",
    )(u, pool_w, pool_scale.reshape(1, n_groups * c))


def _dwconv_kernel(a_ref, gate_ref, w_ref, b_ref, o_ref, pad_ref, *, seq, chunk):
    def glu(t0):
        return a_ref[pl.ds(t0, chunk), :] * jax.nn.sigmoid(gate_ref[pl.ds(t0, chunk), :])

    _fill_padded(pad_ref, seq, glu, chunk)
    w = w_ref[...]
    half = CONV_KSIZE // 2
    span = chunk + 2 * SEQ_PAD - V7X_SUBLANES

    def body(c, carry):
        t0 = pl.multiple_of(c * chunk, chunk)
        blk = pad_ref[pl.ds(t0, chunk + 2 * SEQ_PAD), :]
        acc = jnp.broadcast_to(b_ref[...], (chunk, blk.shape[1]))
        for res in range(V7X_SUBLANES):
            shifted = None
            for k in range(CONV_KSIZE):
                off = SEQ_PAD - half + k
                if off % V7X_SUBLANES != res:
                    continue
                if shifted is None:
                    shifted = blk[res:res + span, :]
                base = off - res
                acc = acc + shifted[base:base + chunk, :] * w[k:k + 1, :]
        o_ref[pl.ds(t0, chunk), :] = acc
        return carry

    lax.fori_loop(0, seq // chunk, body, 0)


def _dwconv(u, a_col, gate_col, dw_w, dw_b, batch, seq):
    ksz, c = dw_w.shape
    tc = V7X_LANES
    chunk = min(128, seq)
    return pl.pallas_call(
        functools.partial(_dwconv_kernel, seq=seq, chunk=chunk),
        out_shape=jax.ShapeDtypeStruct((batch * seq, c), _F32),
        grid=(batch, c // tc),
        in_specs=[
            pl.BlockSpec((seq, tc), lambda b, j: (b, a_col // tc + j)),
            pl.BlockSpec((seq, tc), lambda b, j: (b, gate_col // tc + j)),
            pl.BlockSpec((ksz, tc), lambda b, j: (0, j)),
            pl.BlockSpec((1, tc), lambda b, j: (0, j)),
        ],
        out_specs=pl.BlockSpec((seq, tc), lambda b, j: (b, j)),
        scratch_shapes=[pltpu.VMEM((seq + 2 * SEQ_PAD, tc), _F32)],
        compiler_params=_params("parallel", "parallel"),
        name="dwconv",
    )(u, u, dw_w, dw_b.reshape(1, c))


def _pw_kernel(h_ref, g_ref, w_ref, o_ref):
    x = h_ref[...]
    y = x * lax.rsqrt(jnp.mean(x * x, axis=-1, keepdims=True) + EPS) * g_ref[...]
    y = y * jax.nn.sigmoid(y)
    o_ref[...] = _dot(y.astype(_BF16), w_ref[...]).astype(o_ref.dtype)


def _pw(h, g, w, tm):
    m, c = h.shape
    return pl.pallas_call(
        _pw_kernel,
        out_shape=jax.ShapeDtypeStruct((m, c), _BF16),
        grid=(m // tm,),
        in_specs=[
            pl.BlockSpec((tm, c), lambda i: (i, 0)),
            pl.BlockSpec((1, c), lambda i: (0, 0)),
            pl.BlockSpec((c, c), lambda i: (0, 0)),
        ],
        out_specs=pl.BlockSpec((tm, c), lambda i: (i, 0)),
        compiler_params=_params("parallel"),
        name="conv_pw",
    )(h, g.reshape(1, c), w)


def _outproj_kernel(yp_ref, yc_ref, ya_ref, w_ref, x_ref, gate_ref, o_ref):
    kp = yp_ref.shape[1]
    kc = yc_ref.shape[1]
    acc = _dot(yp_ref[...], w_ref[pl.ds(0, kp), :])
    acc = acc + _dot(yc_ref[...], w_ref[pl.ds(kp, kc), :])
    acc = acc + _dot(ya_ref[...], w_ref[pl.ds(kp + kc, ya_ref.shape[1]), :])
    o_ref[...] = x_ref[...] + gate_ref[...] * acc


def _outproj(yp, yc, ya, w, x, mod, gate_idx, rows, tm, tn):
    m, d = x.shape
    tn = _tile(d, tn)
    lhs = lambda y: pl.BlockSpec((tm, y.shape[1]), lambda i, j: (i, 0))
    return pl.pallas_call(
        _outproj_kernel,
        out_shape=jax.ShapeDtypeStruct((m, d), _F32),
        grid=(m // tm, d // tn),
        in_specs=[
            lhs(yp), lhs(yc), lhs(ya),
            pl.BlockSpec((w.shape[0], tn), lambda i, j: (0, j)),
            pl.BlockSpec((tm, tn), lambda i, j: (i, j)),
            _mod_spec(tn, gate_idx, rows, tm),
        ],
        out_specs=pl.BlockSpec((tm, tn), lambda i, j: (i, j)),
        compiler_params=_params("parallel", "parallel"),
        name="outproj",
    )(yp, yc, ya, w, x, mod)


def _mlp1_kernel(h_ref, w_ref, o_ref):
    a = jnp.maximum(_dot(h_ref[...], w_ref[...]), 0.0)
    o_ref[...] = (a * a).astype(o_ref.dtype)


def _mlp1(h, w, tm, tn):
    m, d = h.shape
    f = w.shape[1]
    tn = _tile(f, tn)
    return pl.pallas_call(
        _mlp1_kernel,
        out_shape=jax.ShapeDtypeStruct((m, f), _BF16),
        grid=(m // tm, f // tn),
        in_specs=[
            pl.BlockSpec((tm, d), lambda i, j: (i, 0)),
            pl.BlockSpec((d, tn), lambda i, j: (0, j)),
        ],
        out_specs=pl.BlockSpec((tm, tn), lambda i, j: (i, j)),
        compiler_params=_params("parallel", "parallel"),
        name="mlp1",
    )(h, w)


def _mlp2_kernel(a_ref, w_ref, x_ref, gate_ref, o_ref, *, nk):
    k = pl.program_id(2)
    part = _dot(a_ref[...], w_ref[...])

    @pl.when(k == 0)
    def _():
        o_ref[...] = part

    @pl.when(k > 0)
    def _():
        o_ref[...] += part

    @pl.when(k == nk - 1)
    def _():
        o_ref[...] = x_ref[...] + gate_ref[...] * o_ref[...]


def _mlp2(a, w, x, mod, gate_idx, rows, tm, tn, tk):
    m, d = x.shape
    f = a.shape[1]
    tn = _tile(d, tn)
    tk = _tile(f, tk)
    nk = f // tk
    return pl.pallas_call(
        functools.partial(_mlp2_kernel, nk=nk),
        out_shape=jax.ShapeDtypeStruct((m, d), _F32),
        grid=(m // tm, d // tn, nk),
        in_specs=[
            pl.BlockSpec((tm, tk), lambda i, j, k: (i, k)),
            pl.BlockSpec((tk, tn), lambda i, j, k: (k, j)),
            pl.BlockSpec((tm, tn), lambda i, j, k: (i, j)),
            _mod_spec(tn, gate_idx, rows, tm),
        ],
        out_specs=pl.BlockSpec((tm, tn), lambda i, j, k: (i, j)),
        compiler_params=_params("parallel", "parallel", "arbitrary"),
        name="mlp2",
    )(a, w, x, mod)


def _mixers(u_pc, y_attn, p, batch, seq, tm):
    pool_width = p["pool_w"].shape[0] * p["pool_w"].shape[1]
    conv_width = p["conv_dw_w"].shape[1]
    y_pool = _pool(u_pc, p["pool_w"], p["pool_scale"], batch, seq)
    hconv = _dwconv(u_pc, pool_width, pool_width + conv_width, p["conv_dw_w"], p["conv_dw_b"], batch, seq)
    y_conv = _pw(hconv, p["conv_norm_g"], p["conv_pw_w"], tm)
    return y_pool, y_conv, y_attn


def _post_attention(x, ys, p, mod, rows, tm):
    x = _outproj(*ys, p["w_out"], x, mod, 2, rows, tm, 512)
    h = _normmod(x, p["norm2_g"], mod, 3, rows, min(tm, 256))
    a = _mlp1(h, p["w_mlp1"], tm, 1024)
    return _mlp2(a, p["w_mlp2"], x, mod, 5, rows, tm, 1024, 2048)


def kernel(x, c, ctx, c_ctx, w_ada, b_ada, norm1_g, norm2_g, w_in, pool_w, pool_scale, conv_dw_w, conv_dw_b,
           conv_norm_g, conv_pw_w, q_norm_g, k_norm_g, rpb, w_out, w_mlp1, w_mlp2):
    batch, seq, d = x.shape
    n_ctx = ctx.shape[1]
    depth = w_in.shape[0]
    n_heads = rpb.shape[1]
    attn_w = n_heads * HEAD_DIM
    pool_width = pool_w.shape[1] * pool_w.shape[2]
    conv_width = conv_dw_w.shape[2]
    off_q = pool_width + 2 * conv_width
    off_k = off_q + attn_w
    kh = min(WIN_H, seq // GRID_W)

    m_lat, m_ctx = batch * seq, batch * n_ctx
    tm_lat, tm_ctx = _tile(seq, 1024), _tile(m_ctx, 1024)
    ctx_row = batch
    lat_row = _Rows(base=0, tokens=seq)
    cx_row = _Rows(base=ctx_row, tokens=m_ctx)

    x = x.reshape(m_lat, d)
    xc = ctx.reshape(m_ctx, d)
    n_rows = -(-(batch + 1) // V7X_SUBLANES) * V7X_SUBLANES
    cvec = jnp.zeros((n_rows, d), _F32).at[:batch].set(c).at[ctx_row].set(c_ctx)
    qk_gain = jnp.concatenate([jnp.tile(q_norm_g, (1, n_heads)), jnp.tile(k_norm_g, (1, n_heads))], axis=1)

    for layer in range(depth):
        p = {
            "w_in": w_in[layer].astype(_BF16), "w_out": w_out[layer].astype(_BF16),
            "w_mlp1": w_mlp1[layer].astype(_BF16), "w_mlp2": w_mlp2[layer].astype(_BF16),
            "pool_w": pool_w[layer].astype(_BF16), "pool_scale": pool_scale[layer],
            "conv_dw_w": conv_dw_w[layer], "conv_dw_b": conv_dw_b[layer],
            "conv_norm_g": conv_norm_g[layer], "conv_pw_w": conv_pw_w[layer].astype(_BF16),
            "norm2_g": norm2_g[layer],
        }
        gains = qk_gain[layer:layer + 1]
        update_ctx = layer < depth - 1
        mod = _ada(cvec, w_ada[layer], b_ada[layer]).reshape(n_rows, N_MOD, 1, d)
        bias = _bias_table(rpb[layer], kh)

        hc = _normmod(xc, norm1_g[layer], mod, 0, cx_row, min(tm_ctx, 256))
        if update_ctx:
            uc_pc = _proj(hc, p["w_in"], 0, off_q, None, _F32, tm_ctx, 1024, "proj_pc")
            qkv_c = _proj(hc, p["w_in"], off_q, 3 * attn_w, gains, _BF16, tm_ctx, 1024, "proj_qkv")
            kc_col, vc_col = attn_w, 2 * attn_w
        else:
            qkv_c = _proj(hc, p["w_in"], off_k, 2 * attn_w, gains[:, attn_w:], _BF16, tm_ctx, 1024, "proj_kv")
            kc_col, vc_col = 0, attn_w

        h = _normmod(x, norm1_g[layer], mod, 0, lat_row, min(tm_lat, 256))
        u_pc = _proj(h, p["w_in"], 0, off_q, None, _F32, tm_lat, 1024, "proj_pc")
        qkv = _proj(h, p["w_in"], off_q, 3 * attn_w, gains, _BF16, tm_lat, 1024, "proj_qkv")
        y_attn = _nattn(qkv, qkv_c, kc_col, vc_col, bias, batch, seq, n_ctx, n_heads)
        x = _post_attention(x, _mixers(u_pc, y_attn, p, batch, seq, tm_lat), p, mod, lat_row, tm_lat)

        if update_ctx:
            yc_attn = _cattn(qkv_c, batch, n_ctx, n_heads)
            xc = _post_attention(xc, _mixers(uc_pc, yc_attn, p, batch, n_ctx, tm_ctx), p, mod, cx_row, tm_ctx)

    return x.reshape(batch, seq, d)
```

```python
import functools
from typing import NamedTuple

import jax
import jax.numpy as jnp
from jax import lax
from jax.experimental import pallas as pl
from jax.experimental.pallas import tpu as pltpu

GRID_W = 64
WIN_H = 8
WIN_W = 16
HEAD_DIM = 128
POOL_WINDOWS = (2, 4, 8, 16)
CONV_KSIZE = 31
N_MOD = 6
EPS = 1e-6
MASK_VALUE = -1e30

V7X_LANES = 128
V7X_SUBLANES = 8
V7X_VMEM_LIMIT_BYTES = 56 * 1024 * 1024
NATTN_PAIRS_PER_STEP = 4
SEQ_PAD = 16

_BF16 = jnp.bfloat16
_F32 = jnp.float32


def _params(*sem):
    return pltpu.CompilerParams(dimension_semantics=sem, vmem_limit_bytes=V7X_VMEM_LIMIT_BYTES)


def _tile(n, pref, *also):
    if n <= pref and not any(a % n for a in also):
        return n
    t = min(pref, n) - min(pref, n) % V7X_LANES
    while n % t or any(a % t for a in also):
        t -= V7X_LANES
    return t


def _dot(a, b):
    return jnp.dot(a, b, preferred_element_type=_F32)


def _cast_kernel(w_ref, o_ref):
    o_ref[...] = w_ref[...].astype(o_ref.dtype)


def _to_bf16(w, layer):
    _, r, c = w.shape
    tr, tc = _tile(r, 512), _tile(c, 2048)
    return pl.pallas_call(
        _cast_kernel,
        out_shape=jax.ShapeDtypeStruct((r, c), _BF16),
        grid=(r // tr, c // tc),
        in_specs=[pl.BlockSpec((None, tr, tc), lambda i, j: (layer, i, j))],
        out_specs=pl.BlockSpec((tr, tc), lambda i, j: (i, j)),
        compiler_params=_params("parallel", "parallel"),
        name="cast_bf16",
    )(w)


def _ada_kernel(c_ref, w_ref, b_ref, o_ref):
    c = c_ref[...]
    s = (c * jax.nn.sigmoid(c)).astype(_BF16)
    o_ref[...] = _dot(s, w_ref[...].astype(_BF16)) + b_ref[...]


def _ada(cvec, w_ada, b_ada, layer):
    r, d = cvec.shape
    n = w_ada.shape[2]
    tn = _tile(n, 1024)
    return pl.pallas_call(
        _ada_kernel,
        out_shape=jax.ShapeDtypeStruct((r, n), _F32),
        grid=(n // tn,),
        in_specs=[
            pl.BlockSpec((r, d), lambda j: (0, 0)),
            pl.BlockSpec((None, d, tn), lambda j: (layer, 0, j)),
            pl.BlockSpec((None, 1, tn), lambda j: (layer, 0, j)),
        ],
        out_specs=pl.BlockSpec((r, tn), lambda j: (0, j)),
        compiler_params=_params("parallel"),
        name="ada",
    )(cvec, w_ada, b_ada.reshape(b_ada.shape[0], 1, n))


class _Rows(NamedTuple):
    base: int
    tokens: int

    def of_block(self, i, tm):
        return self.base + (i * tm) // self.tokens


def _mod_spec(tn, which, rows, tm):
    return pl.BlockSpec((None, None, 1, tn), lambda i, j=0, *_: (rows.of_block(i, tm), which, 0, j))


def _normmod_kernel(x_ref, g_ref, shift_ref, scale_ref, o_ref):
    x = x_ref[...]
    y = x * lax.rsqrt(jnp.mean(x * x, axis=-1, keepdims=True) + EPS) * g_ref[...]
    o_ref[...] = (y * (1.0 + scale_ref[...]) + shift_ref[...]).astype(o_ref.dtype)


def _normmod(x, g, mod, shift_idx, rows, tm):
    m, d = x.shape
    return pl.pallas_call(
        _normmod_kernel,
        out_shape=jax.ShapeDtypeStruct((m, d), _BF16),
        grid=(m // tm,),
        in_specs=[
            pl.BlockSpec((tm, d), lambda i: (i, 0)),
            pl.BlockSpec((1, d), lambda i: (0, 0)),
            _mod_spec(d, shift_idx, rows, tm),
            _mod_spec(d, shift_idx + 1, rows, tm),
        ],
        out_specs=pl.BlockSpec((tm, d), lambda i: (i, 0)),
        compiler_params=_params("parallel"),
        name="normmod",
    )(x, g.reshape(1, d), mod, mod)


def _proj_plain_kernel(h_ref, w_ref, o_ref):
    o_ref[...] = _dot(h_ref[...], w_ref[...]).astype(o_ref.dtype)


def _proj_kernel(h_ref, w_ref, g_ref, o_ref, *, n_norm_blocks):
    acc = _dot(h_ref[...], w_ref[...])
    j = pl.program_id(1)

    @pl.when(j < n_norm_blocks)
    def _():
        for h in range(acc.shape[1] // HEAD_DIM):
            sl = slice(h * HEAD_DIM, (h + 1) * HEAD_DIM)
            a = acc[:, sl]
            y = a * lax.rsqrt(jnp.mean(a * a, axis=-1, keepdims=True) + EPS) * g_ref[:, sl]
            o_ref[:, sl] = y.astype(o_ref.dtype)

    @pl.when(j >= n_norm_blocks)
    def _():
        o_ref[...] = acc.astype(o_ref.dtype)


def _proj(h, w, col0, ncols, gains, out_dtype, tm, tn, name):
    m, d = h.shape
    tn = _tile(ncols, tn, col0, *(() if gains is None else (gains.shape[1],)))
    cb = col0 // tn
    in_specs = [
        pl.BlockSpec((tm, d), lambda i, j: (i, 0)),
        pl.BlockSpec((d, tn), lambda i, j: (0, cb + j)),
    ]
    if gains is None:
        body, args = _proj_plain_kernel, (h, w)
    else:
        n_norm_blocks = gains.shape[1] // tn
        body, args = functools.partial(_proj_kernel, n_norm_blocks=n_norm_blocks), (h, w, gains)
        in_specs.append(pl.BlockSpec((1, tn), lambda i, j: (0, jnp.minimum(j, n_norm_blocks - 1))))
    return pl.pallas_call(
        body,
        out_shape=jax.ShapeDtypeStruct((m, ncols), out_dtype),
        grid=(m // tm, ncols // tn),
        in_specs=in_specs,
        out_specs=pl.BlockSpec((tm, tn), lambda i, j: (i, j)),
        compiler_params=_params("parallel", "parallel"),
        name=name,
    )(*args)


def _nattn_kernel(q_ref, k_ref, v_ref, kc_ref, vc_ref, b_ref, o_ref, *, rows, kh):
    scale = HEAD_DIM ** -0.5
    nu = min(kh + 1, rows)
    nt = (((1,), (1,)), ((), ()))
    tn = (((0,), (0,)), ((), ()))
    kc = kc_ref[...]
    vc = vc_ref[...]
    second = lax.broadcasted_iota(jnp.int32, (1, 2 * GRID_W), 1) >= GRID_W

    def scores(p):
        i = 2 * p
        ustart = jnp.clip(i - kh // 2, 0, rows - nu)
        start0 = jnp.clip(i - kh // 2, 0, rows - kh)
        start1 = jnp.clip(i + 1 - kh // 2, 0, rows - kh)
        lo = jnp.where(second, start1, start0) - ustart
        d0 = ustart - i + (WIN_H - 1)
        qsl = pl.ds(pl.multiple_of(i * GRID_W, 2 * GRID_W), 2 * GRID_W)
        win = pl.ds(pl.multiple_of(ustart * GRID_W, GRID_W), nu * GRID_W)
        q2 = q_ref[qsl, :]
        bias = jnp.concatenate(
            [jnp.where((lo <= r) & (r < lo + kh), b_ref[d0 + r], MASK_VALUE) for r in range(nu)], axis=0)
        s_nb = lax.dot_general(k_ref[win, :], q2, nt, preferred_element_type=_F32) * scale + bias
        s_cx = lax.dot_general(kc, q2, nt, preferred_element_type=_F32) * scale
        return qsl, win, s_nb, s_cx

    def softmax(s_nb, s_cx):
        m = jnp.maximum(jnp.max(s_nb, axis=0, keepdims=True), jnp.max(s_cx, axis=0, keepdims=True))
        e_nb = jnp.exp(s_nb - m)
        e_cx = jnp.exp(s_cx - m)
        inv = 1.0 / (jnp.sum(e_nb, axis=0, keepdims=True) + jnp.sum(e_cx, axis=0, keepdims=True))
        return (e_nb * inv).astype(_BF16), (e_cx * inv).astype(_BF16)

    def step(t, carry):
        staged = [scores(NATTN_PAIRS_PER_STEP * t + g) for g in range(NATTN_PAIRS_PER_STEP)]
        probs = [softmax(s_nb, s_cx) for _, _, s_nb, s_cx in staged]
        for (qsl, win, _, _), (p_nb, p_cx) in zip(staged, probs):
            o = lax.dot_general(p_nb, v_ref[win, :], tn, preferred_element_type=_F32)
            o = o + lax.dot_general(p_cx, vc, tn, preferred_element_type=_F32)
            o_ref[qsl, :] = o.astype(o_ref.dtype)
        return carry

    lax.fori_loop(0, rows // (2 * NATTN_PAIRS_PER_STEP), step, 0)


def _bias_table(rpb):
    j = jnp.arange(GRID_W)
    col_start = jnp.clip(j - WIN_W // 2, 0, GRID_W - WIN_W)
    valid = (j[None, :] >= col_start[:, None]) & (j[None, :] < col_start[:, None] + WIN_W)
    dc = jnp.clip(j[None, :] - j[:, None] + (WIN_W - 1), 0, 2 * WIN_W - 2)
    t = jnp.where(valid[None, None], rpb[:, :, dc], MASK_VALUE)
    t = jnp.swapaxes(t, 2, 3)
    masked = jnp.full_like(t[:, :1], MASK_VALUE)
    first = jnp.concatenate([t, masked], axis=1)
    second = jnp.concatenate([masked, t], axis=1)
    return jnp.concatenate([first, second], axis=-1).astype(_F32)


def _nattn(qkv, kvc, kc_col, vc_col, bias, batch, seq, n_ctx, n_heads):
    rows = seq // GRID_W
    kh = min(WIN_H, rows)
    assert rows % (2 * NATTN_PAIRS_PER_STEP) == 0
    a = n_heads * HEAD_DIM
    hb = lambda off: off // HEAD_DIM
    return pl.pallas_call(
        functools.partial(_nattn_kernel, rows=rows, kh=kh),
        out_shape=jax.ShapeDtypeStruct((batch * seq, a), _BF16),
        grid=(batch, n_heads),
        in_specs=[
            pl.BlockSpec((seq, HEAD_DIM), lambda b, h: (b, h)),
            pl.BlockSpec((seq, HEAD_DIM), lambda b, h: (b, hb(a) + h)),
            pl.BlockSpec((seq, HEAD_DIM), lambda b, h: (b, hb(2 * a) + h)),
            pl.BlockSpec((n_ctx, HEAD_DIM), lambda b, h: (b, hb(kc_col) + h)),
            pl.BlockSpec((n_ctx, HEAD_DIM), lambda b, h: (b, hb(vc_col) + h)),
            pl.BlockSpec((None, 2 * WIN_H, GRID_W, 2 * GRID_W), lambda b, h: (h, 0, 0, 0)),
        ],
        out_specs=pl.BlockSpec((seq, HEAD_DIM), lambda b, h: (b, h)),
        compiler_params=_params("parallel", "parallel"),
        name="nattn",
    )(qkv, qkv, qkv, kvc, kvc, bias)


def _cattn_kernel(q_ref, k_ref, v_ref, o_ref):
    scale = HEAD_DIM ** -0.5
    s = lax.dot_general(q_ref[...], k_ref[...], (((1,), (1,)), ((), ())), preferred_element_type=_F32) * scale
    e = jnp.exp(s - jnp.max(s, axis=-1, keepdims=True))
    p = e * (1.0 / jnp.sum(e, axis=-1, keepdims=True))
    o_ref[...] = _dot(p.astype(_BF16), v_ref[...]).astype(o_ref.dtype)


def _cattn(qkv, batch, n_ctx, n_heads):
    a = n_heads * HEAD_DIM
    spec = lambda off: pl.BlockSpec((n_ctx, HEAD_DIM), lambda b, h: (b, off // HEAD_DIM + h))
    return pl.pallas_call(
        _cattn_kernel,
        out_shape=jax.ShapeDtypeStruct((batch * n_ctx, a), _BF16),
        grid=(batch, n_heads),
        in_specs=[spec(0), spec(a), spec(2 * a)],
        out_specs=spec(0),
        compiler_params=_params("parallel", "parallel"),
        name="cattn",
    )(qkv, qkv, qkv)


def _fill_padded(pad_ref, seq, fill_chunk, chunk):
    zeros = jnp.zeros((SEQ_PAD, pad_ref.shape[1]), pad_ref.dtype)
    pad_ref[pl.ds(0, SEQ_PAD), :] = zeros
    pad_ref[pl.ds(SEQ_PAD + seq, SEQ_PAD), :] = zeros

    def body(c, carry):
        t0 = pl.multiple_of(c * chunk, chunk)
        pad_ref[pl.ds(SEQ_PAD + t0, chunk), :] = fill_chunk(t0)
        return carry

    lax.fori_loop(0, seq // chunk, body, 0)


def _pool_kernel(u_ref, w_ref, s_ref, o_ref, pad_ref, *, seq, chunk):
    g = pl.program_id(1)
    _fill_padded(pad_ref, seq, lambda t0: u_ref[pl.ds(t0, chunk), :], chunk)
    w = w_ref[...]
    s = s_ref[...]

    for gi, win in enumerate(POOL_WINDOWS):

        @pl.when(g == gi)
        def _(win=win):
            def body(c, carry):
                t0 = pl.multiple_of(c * chunk, chunk)
                blk = pad_ref[pl.ds(t0, chunk + 2 * SEQ_PAD), :]
                tot = None
                for sft in range(-(win // 2), win - win // 2):
                    part = blk[SEQ_PAD + sft:SEQ_PAD + sft + chunk, :]
                    tot = part if tot is None else tot + part
                t = t0 + lax.broadcasted_iota(jnp.int32, (chunk, blk.shape[1]), 0)
                lo = jnp.maximum(t - win // 2, 0)
                hi = jnp.minimum(t + (win - 1 - win // 2), seq - 1)
                cnt = (hi - lo + 1).astype(_F32)
                pooled = tot / cnt - blk[SEQ_PAD:SEQ_PAD + chunk, :]
                o_ref[pl.ds(t0, chunk), :] = (_dot(pooled.astype(_BF16), w) * s).astype(o_ref.dtype)
                return carry

            lax.fori_loop(0, seq // chunk, body, 0)


def _pool(u, pool_w, pool_scale, batch, seq):
    n_groups, c, _ = pool_w.shape
    chunk = min(128, seq)
    return pl.pallas_call(
        functools.partial(_pool_kernel, seq=seq, chunk=chunk),
        out_shape=jax.ShapeDtypeStruct((batch * seq, n_groups * c), _BF16),
        grid=(batch, n_groups),
        in_specs=[
            pl.BlockSpec((seq, c), lambda b, g: (b, g)),
            pl.BlockSpec((None, c, c), lambda b, g: (g, 0, 0)),
            pl.BlockSpec((1, c), lambda b, g: (0, g)),
        ],
        out_specs=pl.BlockSpec((seq, c), lambda b, g: (b, g)),
        scratch_shapes=[pltpu.VMEM((seq + 2 * SEQ_PAD, c), _F32)],
        compiler_params=_params("parallel", "parallel"),
        name="pool",
    )(u, pool_w, pool_scale.reshape(1, n_groups * c))


def _dwconv_kernel(a_ref, gate_ref, w_ref, b_ref, o_ref, pad_ref, *, seq, chunk):
    def glu(t0):
        return a_ref[pl.ds(t0, chunk), :] * jax.nn.sigmoid(gate_ref[pl.ds(t0, chunk), :])

    _fill_padded(pad_ref, seq, glu, chunk)
    w = w_ref[...]
    half = CONV_KSIZE // 2
    span = chunk + 2 * SEQ_PAD - V7X_SUBLANES

    def body(c, carry):
        t0 = pl.multiple_of(c * chunk, chunk)
        blk = pad_ref[pl.ds(t0, chunk + 2 * SEQ_PAD), :]
        acc = jnp.broadcast_to(b_ref[...], (chunk, blk.shape[1]))
        for res in range(V7X_SUBLANES):
            shifted = None
            for k in range(CONV_KSIZE):
                off = SEQ_PAD - half + k
                if off % V7X_SUBLANES != res:
                    continue
                if shifted is None:
                    shifted = blk[res:res + span, :]
                base = off - res
                acc = acc + shifted[base:base + chunk, :] * w[k:k + 1, :]
        o_ref[pl.ds(t0, chunk), :] = acc
        return carry

    lax.fori_loop(0, seq // chunk, body, 0)


def _dwconv(u, a_col, gate_col, dw_w, dw_b, batch, seq):
    ksz, c = dw_w.shape
    tc = V7X_LANES
    chunk = min(128, seq)
    return pl.pallas_call(
        functools.partial(_dwconv_kernel, seq=seq, chunk=chunk),
        out_shape=jax.ShapeDtypeStruct((batch * seq, c), _F32),
        grid=(batch, c // tc),
        in_specs=[
            pl.BlockSpec((seq, tc), lambda b, j: (b, a_col // tc + j)),
            pl.BlockSpec((seq, tc), lambda b, j: (b, gate_col // tc + j)),
            pl.BlockSpec((ksz, tc), lambda b, j: (0, j)),
            pl.BlockSpec((1, tc), lambda b, j: (0, j)),
        ],
        out_specs=pl.BlockSpec((seq, tc), lambda b, j: (b, j)),
        scratch_shapes=[pltpu.VMEM((seq + 2 * SEQ_PAD, tc), _F32)],
        compiler_params=_params("parallel", "parallel"),
        name="dwconv",
    )(u, u, dw_w, dw_b.reshape(1, c))


def _pw_kernel(h_ref, g_ref, w_ref, o_ref):
    x = h_ref[...]
    y = x * lax.rsqrt(jnp.mean(x * x, axis=-1, keepdims=True) + EPS) * g_ref[...]
    y = y * jax.nn.sigmoid(y)
    o_ref[...] = _dot(y.astype(_BF16), w_ref[...]).astype(o_ref.dtype)


def _pw(h, g, w, tm):
    m, c = h.shape
    return pl.pallas_call(
        _pw_kernel,
        out_shape=jax.ShapeDtypeStruct((m, c), _BF16),
        grid=(m // tm,),
        in_specs=[
            pl.BlockSpec((tm, c), lambda i: (i, 0)),
            pl.BlockSpec((1, c), lambda i: (0, 0)),
            pl.BlockSpec((c, c), lambda i: (0, 0)),
        ],
        out_specs=pl.BlockSpec((tm, c), lambda i: (i, 0)),
        compiler_params=_params("parallel"),
        name="conv_pw",
    )(h, g.reshape(1, c), w)


def _outproj_kernel(yp_ref, yc_ref, ya_ref, w_ref, x_ref, gate_ref, o_ref):
    kp = yp_ref.shape[1]
    kc = yc_ref.shape[1]
    acc = _dot(yp_ref[...], w_ref[pl.ds(0, kp), :])
    acc = acc + _dot(yc_ref[...], w_ref[pl.ds(kp, kc), :])
    acc = acc + _dot(ya_ref[...], w_ref[pl.ds(kp + kc, ya_ref.shape[1]), :])
    o_ref[...] = x_ref[...] + gate_ref[...] * acc


def _outproj(yp, yc, ya, w, x, mod, gate_idx, rows, tm, tn):
    m, d = x.shape
    tn = _tile(d, tn)
    lhs = lambda y: pl.BlockSpec((tm, y.shape[1]), lambda i, j: (i, 0))
    return pl.pallas_call(
        _outproj_kernel,
        out_shape=jax.ShapeDtypeStruct((m, d), _F32),
        grid=(m // tm, d // tn),
        in_specs=[
            lhs(yp), lhs(yc), lhs(ya),
            pl.BlockSpec((w.shape[0], tn), lambda i, j: (0, j)),
            pl.BlockSpec((tm, tn), lambda i, j: (i, j)),
            _mod_spec(tn, gate_idx, rows, tm),
        ],
        out_specs=pl.BlockSpec((tm, tn), lambda i, j: (i, j)),
        compiler_params=_params("parallel", "parallel"),
        name="outproj",
    )(yp, yc, ya, w, x, mod)


def _mlp1_kernel(h_ref, w_ref, o_ref):
    a = jnp.maximum(_dot(h_ref[...], w_ref[...]), 0.0)
    o_ref[...] = (a * a).astype(o_ref.dtype)


def _mlp1(h, w, tm, tn):
    m, d = h.shape
    f = w.shape[1]
    tn = _tile(f, tn)
    return pl.pallas_call(
        _mlp1_kernel,
        out_shape=jax.ShapeDtypeStruct((m, f), _BF16),
        grid=(m // tm, f // tn),
        in_specs=[
            pl.BlockSpec((tm, d), lambda i, j: (i, 0)),
            pl.BlockSpec((d, tn), lambda i, j: (0, j)),
        ],
        out_specs=pl.BlockSpec((tm, tn), lambda i, j: (i, j)),
        compiler_params=_params("parallel", "parallel"),
        name="mlp1",
    )(h, w)


def _mlp2_kernel(a_ref, w_ref, x_ref, gate_ref, o_ref, *, nk):
    k = pl.program_id(2)
    part = _dot(a_ref[...], w_ref[...])

    @pl.when(k == 0)
    def _():
        o_ref[...] = part

    @pl.when(k > 0)
    def _():
        o_ref[...] += part

    @pl.when(k == nk - 1)
    def _():
        o_ref[...] = x_ref[...] + gate_ref[...] * o_ref[...]


def _mlp2(a, w, x, mod, gate_idx, rows, tm, tn, tk):
    m, d = x.shape
    f = a.shape[1]
    tn = _tile(d, tn)
    tk = _tile(f, tk)
    nk = f // tk
    return pl.pallas_call(
        functools.partial(_mlp2_kernel, nk=nk),
        out_shape=jax.ShapeDtypeStruct((m, d), _F32),
        grid=(m // tm, d // tn, nk),
        in_specs=[
            pl.BlockSpec((tm, tk), lambda i, j, k: (i, k)),
            pl.BlockSpec((tk, tn), lambda i, j, k: (k, j)),
            pl.BlockSpec((tm, tn), lambda i, j, k: (i, j), pipeline_mode=pl.Buffered(1)),
            _mod_spec(tn, gate_idx, rows, tm),
        ],
        out_specs=pl.BlockSpec((tm, tn), lambda i, j, k: (i, j)),
        compiler_params=_params("parallel", "parallel", "arbitrary"),
        name="mlp2",
    )(a, w, x, mod)


def _mixers(u_pc, y_attn, p, batch, seq, tm):
    pool_width = p["pool_w"].shape[0] * p["pool_w"].shape[1]
    conv_width = p["conv_dw_w"].shape[1]
    y_pool = _pool(u_pc, p["pool_w"], p["pool_scale"], batch, seq)
    hconv = _dwconv(u_pc, pool_width, pool_width + conv_width, p["conv_dw_w"], p["conv_dw_b"], batch, seq)
    y_conv = _pw(hconv, p["conv_norm_g"], p["conv_pw_w"], tm)
    return y_pool, y_conv, y_attn


def _post_attention(x, ys, p, mod, rows, tm):
    x = _outproj(*ys, p["w_out"], x, mod, 2, rows, tm, 512)
    h = _normmod(x, p["norm2_g"], mod, 3, rows, min(tm, 256))
    a = _mlp1(h, p["w_mlp1"], tm, 1024)
    return _mlp2(a, p["w_mlp2"], x, mod, 5, rows, tm, 1024, 4096)


def kernel(x, c, ctx, c_ctx, w_ada, b_ada, norm1_g, norm2_g, w_in, pool_w, pool_scale, conv_dw_w, conv_dw_b,
           conv_norm_g, conv_pw_w, q_norm_g, k_norm_g, rpb, w_out, w_mlp1, w_mlp2):
    batch, seq, d = x.shape
    n_ctx = ctx.shape[1]
    depth = w_in.shape[0]
    n_heads = rpb.shape[1]
    attn_w = n_heads * HEAD_DIM
    pool_width = pool_w.shape[1] * pool_w.shape[2]
    conv_width = conv_dw_w.shape[2]
    off_q = pool_width + 2 * conv_width
    off_k = off_q + attn_w
    kh = min(WIN_H, seq // GRID_W)

    m_lat, m_ctx = batch * seq, batch * n_ctx
    tm_lat, tm_ctx = _tile(seq, 1024), _tile(m_ctx, 1024)
    ctx_row = batch
    lat_row = _Rows(base=0, tokens=seq)
    cx_row = _Rows(base=ctx_row, tokens=m_ctx)

    x = x.reshape(m_lat, d)
    xc = ctx.reshape(m_ctx, d)
    n_rows = -(-(batch + 1) // V7X_SUBLANES) * V7X_SUBLANES
    cvec = jnp.zeros((n_rows, d), _F32).at[:batch].set(c).at[ctx_row].set(c_ctx)
    qk_gain = jnp.concatenate([jnp.tile(q_norm_g, (1, n_heads)), jnp.tile(k_norm_g, (1, n_heads))], axis=1)

    for layer in range(depth):
        p = {
            "w_in": _to_bf16(w_in, layer), "w_out": _to_bf16(w_out, layer),
            "w_mlp1": _to_bf16(w_mlp1, layer), "w_mlp2": _to_bf16(w_mlp2, layer),
            "pool_w": _to_bf16(pool_w.reshape(depth, -1, pool_w.shape[-1]), layer).reshape(pool_w.shape[1:]),
            "pool_scale": pool_scale[layer],
            "conv_dw_w": conv_dw_w[layer], "conv_dw_b": conv_dw_b[layer],
            "conv_norm_g": conv_norm_g[layer], "conv_pw_w": _to_bf16(conv_pw_w, layer),
            "norm2_g": norm2_g[layer],
        }
        gains = qk_gain[layer:layer + 1]
        update_ctx = layer < depth - 1
        mod = _ada(cvec, w_ada, b_ada, layer).reshape(n_rows, N_MOD, 1, d)
        bias = _bias_table(rpb[layer])

        hc = _normmod(xc, norm1_g[layer], mod, 0, cx_row, min(tm_ctx, 256))
        if update_ctx:
            uc_pc = _proj(hc, p["w_in"], 0, off_q, None, _F32, tm_ctx, 1024, "proj_pc")
            qkv_c = _proj(hc, p["w_in"], off_q, 3 * attn_w, gains, _BF16, tm_ctx, 1024, "proj_qkv")
            kc_col, vc_col = attn_w, 2 * attn_w
        else:
            qkv_c = _proj(hc, p["w_in"], off_k, 2 * attn_w, gains[:, attn_w:], _BF16, tm_ctx, 1024, "proj_kv")
            kc_col, vc_col = 0, attn_w

        h = _normmod(x, norm1_g[layer], mod, 0, lat_row, min(tm_lat, 256))
        u_pc = _proj(h, p["w_in"], 0, off_q, None, _F32, tm_lat, 1024, "proj_pc")
        qkv = _proj(h, p["w_in"], off_q, 3 * attn_w, gains, _BF16, tm_lat, 1024, "proj_qkv")
        y_attn = _nattn(qkv, qkv_c, kc_col, vc_col, bias, batch, seq, n_ctx, n_heads)
        x = _post_attention(x, _mixers(u_pc, y_attn, p, batch, seq, tm_lat), p, mod, lat_row, tm_lat)

        if update_ctx:
            yc_attn = _cattn(qkv_c, batch, n_ctx, n_heads)
            xc = _post_attention(xc, _mixers(uc_pc, yc_attn, p, batch, n_ctx, tm_ctx), p, mod, cx_row, tm_ctx)

    return x.reshape(batch, seq, d)
```

```python
import functools
from typing import NamedTuple

import jax
import jax.numpy as jnp
from jax import lax
from jax.experimental import pallas as pl
from jax.experimental.pallas import tpu as pltpu

GRID_W = 64
WIN_H = 8
WIN_W = 16
HEAD_DIM = 128
POOL_WINDOWS = (2, 4, 8, 16)
CONV_KSIZE = 31
N_MOD = 6
EPS = 1e-6
MASK_VALUE = -1e30

V7X_LANES = 128
V7X_SUBLANES = 8
V7X_MXU_COLS = 256
V7X_VMEM_LIMIT_BYTES = 56 * 1024 * 1024
NATTN_PAIRS_PER_STEP = 4
SEQ_PAD = 16

_BF16 = jnp.bfloat16
_F32 = jnp.float32


def _params(*sem):
    return pltpu.CompilerParams(dimension_semantics=sem, vmem_limit_bytes=V7X_VMEM_LIMIT_BYTES)


def _tile(n, pref, *also):
    if n <= pref and not any(a % n for a in also):
        return n
    t = min(pref, n) - min(pref, n) % V7X_LANES
    while n % t or any(a % t for a in also):
        t -= V7X_LANES
    return t


def _dot(a, b):
    return jnp.dot(a, b, preferred_element_type=_F32)


def _cast_kernel(w_ref, o_ref):
    o_ref[...] = w_ref[...].astype(o_ref.dtype)


def _to_bf16(w, layer):
    _, r, c = w.shape
    tr, tc = _tile(r, 512), _tile(c, 2048)
    return pl.pallas_call(
        _cast_kernel,
        out_shape=jax.ShapeDtypeStruct((r, c), _BF16),
        grid=(r // tr, c // tc),
        in_specs=[pl.BlockSpec((None, tr, tc), lambda i, j: (layer, i, j))],
        out_specs=pl.BlockSpec((tr, tc), lambda i, j: (i, j)),
        compiler_params=_params("parallel", "parallel"),
        name="cast_bf16",
    )(w)


class _SideCast(NamedTuple):
    w: jax.Array
    layer: int


def _side_cast_plan(side, grid):
    _, r, c = side.w.shape
    bf16_rows = 2 * V7X_SUBLANES
    nb = max(b for b in range(1, grid[0] * grid[1] + 1) if r % b == 0 and (r // b) % bf16_rows == 0)
    tr = r // nb

    def blk(i, j):
        return jnp.minimum(i * grid[1] + j, nb - 1)

    return (nb,
            pl.BlockSpec((None, tr, c), lambda i, j, *_: (side.layer, blk(i, j), 0)),
            pl.BlockSpec((tr, c), lambda i, j, *_: (blk(i, j), 0)),
            jax.ShapeDtypeStruct((r, c), _BF16))


def _call_with_side_casts(body, args, sides, *, out_shape, grid, in_specs, out_specs, name):
    plans = [_side_cast_plan(s, grid) for s in sides]
    n_in, n_side = len(args), len(sides)

    def wrapped(*refs):
        body(*refs[:n_in], refs[n_in + n_side])
        step = pl.program_id(0) * grid[1] + pl.program_id(1)
        for (nb, _, _, _), src, dst in zip(plans, refs[n_in:n_in + n_side], refs[n_in + n_side + 1:]):
            @pl.when(step < nb)
            def _(src=src, dst=dst):
                dst[...] = src[...].astype(dst.dtype)

    outs = pl.pallas_call(
        wrapped,
        out_shape=[out_shape] + [p[3] for p in plans],
        grid=grid,
        in_specs=list(in_specs) + [p[1] for p in plans],
        out_specs=[out_specs] + [p[2] for p in plans],
        compiler_params=_params(*("arbitrary" if sides else "parallel",) * len(grid)),
        name=name,
    )(*args, *[s.w for s in sides])
    return outs[0], list(outs[1:])


def _ada_kernel(c_ref, w_ref, b_ref, o_ref):
    c = c_ref[...]
    s = (c * jax.nn.sigmoid(c)).astype(_BF16)
    o_ref[...] = _dot(s, w_ref[...].astype(_BF16)) + b_ref[...]


def _ada(cvec, w_ada, b_ada, layer):
    r, d = cvec.shape
    n = w_ada.shape[2]
    tn = _tile(n, 1024)
    return pl.pallas_call(
        _ada_kernel,
        out_shape=jax.ShapeDtypeStruct((r, n), _F32),
        grid=(n // tn,),
        in_specs=[
            pl.BlockSpec((r, d), lambda j: (0, 0)),
            pl.BlockSpec((None, d, tn), lambda j: (layer, 0, j)),
            pl.BlockSpec((None, 1, tn), lambda j: (layer, 0, j)),
        ],
        out_specs=pl.BlockSpec((r, tn), lambda j: (0, j)),
        compiler_params=_params("parallel"),
        name="ada",
    )(cvec, w_ada, b_ada.reshape(b_ada.shape[0], 1, n))


class _Rows(NamedTuple):
    base: int
    tokens: int

    def of_block(self, i, tm):
        return self.base + (i * tm) // self.tokens


def _mod_spec(tn, which, rows, tm):
    return pl.BlockSpec((None, None, 1, tn), lambda i, j=0, *_: (rows.of_block(i, tm), which, 0, j))


def _normmod_kernel(x_ref, g_ref, shift_ref, scale_ref, o_ref):
    x = x_ref[...]
    y = x * lax.rsqrt(jnp.mean(x * x, axis=-1, keepdims=True) + EPS) * g_ref[...]
    o_ref[...] = (y * (1.0 + scale_ref[...]) + shift_ref[...]).astype(o_ref.dtype)


def _normmod(x, g, mod, shift_idx, rows, tm):
    m, d = x.shape
    return pl.pallas_call(
        _normmod_kernel,
        out_shape=jax.ShapeDtypeStruct((m, d), _BF16),
        grid=(m // tm,),
        in_specs=[
            pl.BlockSpec((tm, d), lambda i: (i, 0)),
            pl.BlockSpec((1, d), lambda i: (0, 0)),
            _mod_spec(d, shift_idx, rows, tm),
            _mod_spec(d, shift_idx + 1, rows, tm),
        ],
        out_specs=pl.BlockSpec((tm, d), lambda i: (i, 0)),
        compiler_params=_params("parallel"),
        name="normmod",
    )(x, g.reshape(1, d), mod, mod)


def _col_chunks(n):
    c = V7X_MXU_COLS if n % V7X_MXU_COLS == 0 else n
    return [slice(s, s + c) for s in range(0, n, c)]


def _proj_plain_kernel(h_ref, w_ref, o_ref):
    h = h_ref[...]
    for sl in _col_chunks(o_ref.shape[1]):
        o_ref[:, sl] = _dot(h, w_ref[:, sl]).astype(o_ref.dtype)


def _proj_kernel(h_ref, w_ref, g_ref, o_ref, *, n_norm_blocks):
    normed = pl.program_id(1) < n_norm_blocks
    h = h_ref[...]
    for sl in _col_chunks(o_ref.shape[1]):
        acc = _dot(h, w_ref[:, sl])
        for c in range(0, acc.shape[1], HEAD_DIM):
            a = acc[:, c:c + HEAD_DIM]
            rstd = lax.rsqrt(jnp.mean(a * a, axis=-1, keepdims=True) + EPS)
            gain = g_ref[:, sl.start + c:sl.start + c + HEAD_DIM]
            y = a * jnp.where(normed, rstd, 1.0) * jnp.where(normed, gain, 1.0)
            o_ref[:, sl.start + c:sl.start + c + HEAD_DIM] = y.astype(o_ref.dtype)


def _proj(h, w, col0, ncols, gains, out_dtype, tm, tn, name, sides=()):
    m, d = h.shape
    tn = _tile(ncols, tn, col0, *(() if gains is None else (gains.shape[1],)))
    cb = col0 // tn
    in_specs = [
        pl.BlockSpec((tm, d), lambda i, j: (i, 0)),
        pl.BlockSpec((d, tn), lambda i, j: (0, cb + j)),
    ]
    if gains is None:
        body, args = _proj_plain_kernel, (h, w)
    else:
        n_norm_blocks = gains.shape[1] // tn
        body, args = functools.partial(_proj_kernel, n_norm_blocks=n_norm_blocks), (h, w, gains)
        in_specs.append(pl.BlockSpec((1, tn), lambda i, j: (0, jnp.minimum(j, n_norm_blocks - 1))))
    return _call_with_side_casts(
        body, args, sides,
        out_shape=jax.ShapeDtypeStruct((m, ncols), out_dtype),
        grid=(m // tm, ncols // tn),
        in_specs=in_specs,
        out_specs=pl.BlockSpec((tm, tn), lambda i, j: (i, j)),
        name=name,
    )


def _nattn_kernel(q_ref, k_ref, v_ref, kc_ref, vc_ref, b_ref, o_ref, *, rows, kh):
    scale = HEAD_DIM ** -0.5
    nu = min(kh + 1, rows)
    nt = (((1,), (1,)), ((), ()))
    tn = (((0,), (0,)), ((), ()))
    kc = kc_ref[...]
    vc = vc_ref[...]
    second = lax.broadcasted_iota(jnp.int32, (1, 2 * GRID_W), 1) >= GRID_W

    def scores(p):
        i = 2 * p
        ustart = jnp.clip(i - kh // 2, 0, rows - nu)
        start0 = jnp.clip(i - kh // 2, 0, rows - kh)
        start1 = jnp.clip(i + 1 - kh // 2, 0, rows - kh)
        lo = jnp.where(second, start1, start0) - ustart
        d0 = ustart - i + (WIN_H - 1)
        qsl = pl.ds(pl.multiple_of(i * GRID_W, 2 * GRID_W), 2 * GRID_W)
        win = pl.ds(pl.multiple_of(ustart * GRID_W, GRID_W), nu * GRID_W)
        q2 = q_ref[qsl, :]
        bias = jnp.concatenate(
            [jnp.where((lo <= r) & (r < lo + kh), b_ref[d0 + r], MASK_VALUE) for r in range(nu)], axis=0)
        s_nb = lax.dot_general(k_ref[win, :], q2, nt, preferred_element_type=_F32) * scale + bias
        s_cx = lax.dot_general(kc, q2, nt, preferred_element_type=_F32) * scale
        return qsl, win, s_nb, s_cx

    def softmax(s_nb, s_cx):
        m = jnp.maximum(jnp.max(s_nb, axis=0, keepdims=True), jnp.max(s_cx, axis=0, keepdims=True))
        e_nb = jnp.exp(s_nb - m)
        e_cx = jnp.exp(s_cx - m)
        inv = 1.0 / (jnp.sum(e_nb, axis=0, keepdims=True) + jnp.sum(e_cx, axis=0, keepdims=True))
        return (e_nb * inv).astype(_BF16), (e_cx * inv).astype(_BF16)

    def step(t, carry):
        staged = [scores(NATTN_PAIRS_PER_STEP * t + g) for g in range(NATTN_PAIRS_PER_STEP)]
        probs = [softmax(s_nb, s_cx) for _, _, s_nb, s_cx in staged]
        for (qsl, win, _, _), (p_nb, p_cx) in zip(staged, probs):
            o = lax.dot_general(p_nb, v_ref[win, :], tn, preferred_element_type=_F32)
            o = o + lax.dot_general(p_cx, vc, tn, preferred_element_type=_F32)
            o_ref[qsl, :] = o.astype(o_ref.dtype)
        return carry

    lax.fori_loop(0, rows // (2 * NATTN_PAIRS_PER_STEP), step, 0)


def _bias_table(rpb):
    j = jnp.arange(GRID_W)
    col_start = jnp.clip(j - WIN_W // 2, 0, GRID_W - WIN_W)
    valid = (j[None, :] >= col_start[:, None]) & (j[None, :] < col_start[:, None] + WIN_W)
    dc = jnp.clip(j[None, :] - j[:, None] + (WIN_W - 1), 0, 2 * WIN_W - 2)
    t = jnp.where(valid[None, None], rpb[:, :, dc], MASK_VALUE)
    t = jnp.swapaxes(t, 2, 3)
    masked = jnp.full_like(t[:, :1], MASK_VALUE)
    first = jnp.concatenate([t, masked], axis=1)
    second = jnp.concatenate([masked, t], axis=1)
    return jnp.concatenate([first, second], axis=-1).astype(_F32)


def _nattn(qkv, kvc, kc_col, vc_col, bias, batch, seq, n_ctx, n_heads):
    rows = seq // GRID_W
    kh = min(WIN_H, rows)
    assert rows % (2 * NATTN_PAIRS_PER_STEP) == 0
    a = n_heads * HEAD_DIM
    hb = lambda off: off // HEAD_DIM
    return pl.pallas_call(
        functools.partial(_nattn_kernel, rows=rows, kh=kh),
        out_shape=jax.ShapeDtypeStruct((batch * seq, a), _BF16),
        grid=(batch, n_heads),
        in_specs=[
            pl.BlockSpec((seq, HEAD_DIM), lambda b, h: (b, h)),
            pl.BlockSpec((seq, HEAD_DIM), lambda b, h: (b, hb(a) + h)),
            pl.BlockSpec((seq, HEAD_DIM), lambda b, h: (b, hb(2 * a) + h)),
            pl.BlockSpec((n_ctx, HEAD_DIM), lambda b, h: (b, hb(kc_col) + h)),
            pl.BlockSpec((n_ctx, HEAD_DIM), lambda b, h: (b, hb(vc_col) + h)),
            pl.BlockSpec((None, 2 * WIN_H, GRID_W, 2 * GRID_W), lambda b, h: (h, 0, 0, 0)),
        ],
        out_specs=pl.BlockSpec((seq, HEAD_DIM), lambda b, h: (b, h)),
        compiler_params=_params("parallel", "parallel"),
        name="nattn",
    )(qkv, qkv, qkv, kvc, kvc, bias)


def _cattn_kernel(q_ref, k_ref, v_ref, o_ref):
    scale = HEAD_DIM ** -0.5
    s = lax.dot_general(q_ref[...], k_ref[...], (((1,), (1,)), ((), ())), preferred_element_type=_F32) * scale
    e = jnp.exp(s - jnp.max(s, axis=-1, keepdims=True))
    p = e * (1.0 / jnp.sum(e, axis=-1, keepdims=True))
    o_ref[...] = _dot(p.astype(_BF16), v_ref[...]).astype(o_ref.dtype)


def _cattn(qkv, batch, n_ctx, n_heads):
    a = n_heads * HEAD_DIM
    spec = lambda off: pl.BlockSpec((n_ctx, HEAD_DIM), lambda b, h: (b, off // HEAD_DIM + h))
    return pl.pallas_call(
        _cattn_kernel,
        out_shape=jax.ShapeDtypeStruct((batch * n_ctx, a), _BF16),
        grid=(batch, n_heads),
        in_specs=[spec(0), spec(a), spec(2 * a)],
        out_specs=spec(0),
        compiler_params=_params("parallel", "parallel"),
        name="cattn",
    )(qkv, qkv, qkv)


def _fill_padded(pad_ref, seq, fill_chunk, chunk):
    zeros = jnp.zeros((SEQ_PAD, pad_ref.shape[1]), pad_ref.dtype)
    pad_ref[pl.ds(0, SEQ_PAD), :] = zeros
    pad_ref[pl.ds(SEQ_PAD + seq, SEQ_PAD), :] = zeros

    def body(c, carry):
        t0 = pl.multiple_of(c * chunk, chunk)
        pad_ref[pl.ds(SEQ_PAD + t0, chunk), :] = fill_chunk(t0)
        return carry

    lax.fori_loop(0, seq // chunk, body, 0)


def _pool_kernel(u_ref, w_ref, s_ref, o_ref, pad_ref, *, seq, chunk):
    g = pl.program_id(1)
    _fill_padded(pad_ref, seq, lambda t0: u_ref[pl.ds(t0, chunk), :], chunk)
    w = w_ref[...]
    s = s_ref[...]

    for gi, win in enumerate(POOL_WINDOWS):

        @pl.when(g == gi)
        def _(win=win):
            def body(c, carry):
                t0 = pl.multiple_of(c * chunk, chunk)
                blk = pad_ref[pl.ds(t0, chunk + 2 * SEQ_PAD), :]
                tot = None
                for sft in range(-(win // 2), win - win // 2):
                    part = blk[SEQ_PAD + sft:SEQ_PAD + sft + chunk, :]
                    tot = part if tot is None else tot + part
                t = t0 + lax.broadcasted_iota(jnp.int32, (chunk, blk.shape[1]), 0)
                lo = jnp.maximum(t - win // 2, 0)
                hi = jnp.minimum(t + (win - 1 - win // 2), seq - 1)
                cnt = (hi - lo + 1).astype(_F32)
                pooled = tot / cnt - blk[SEQ_PAD:SEQ_PAD + chunk, :]
                o_ref[pl.ds(t0, chunk), :] = (_dot(pooled.astype(_BF16), w) * s).astype(o_ref.dtype)
                return carry

            lax.fori_loop(0, seq // chunk, body, 0)


def _pool(u, pool_w, pool_scale, batch, seq):
    n_groups, c, _ = pool_w.shape
    chunk = min(128, seq)
    return pl.pallas_call(
        functools.partial(_pool_kernel, seq=seq, chunk=chunk),
        out_shape=jax.ShapeDtypeStruct((batch * seq, n_groups * c), _BF16),
        grid=(batch, n_groups),
        in_specs=[
            pl.BlockSpec((seq, c), lambda b, g: (b, g)),
            pl.BlockSpec((None, c, c), lambda b, g: (g, 0, 0)),
            pl.BlockSpec((1, c), lambda b, g: (0, g)),
        ],
        out_specs=pl.BlockSpec((seq, c), lambda b, g: (b, g)),
        scratch_shapes=[pltpu.VMEM((seq + 2 * SEQ_PAD, c), _F32)],
        compiler_params=_params("parallel", "parallel"),
        name="pool",
    )(u, pool_w, pool_scale.reshape(1, n_groups * c))


def _dwconv_kernel(a_ref, gate_ref, w_ref, b_ref, o_ref, pad_ref, wb_ref, *, seq, chunk):
    def glu(t0):
        return a_ref[pl.ds(t0, chunk), :] * jax.nn.sigmoid(gate_ref[pl.ds(t0, chunk), :])

    _fill_padded(pad_ref, seq, glu, chunk)
    n_ch = o_ref.shape[1]
    for k in range(CONV_KSIZE):
        wb_ref[pl.ds(V7X_SUBLANES * k, V7X_SUBLANES), :] = jnp.broadcast_to(w_ref[k:k + 1, :], (V7X_SUBLANES, n_ch))
    wb_ref[pl.ds(V7X_SUBLANES * CONV_KSIZE, V7X_SUBLANES), :] = jnp.broadcast_to(b_ref[...], (V7X_SUBLANES, n_ch))
    half = CONV_KSIZE // 2
    span = chunk + 2 * SEQ_PAD - V7X_SUBLANES
    groups = chunk // V7X_SUBLANES

    def tap(k):
        return wb_ref[pl.ds(V7X_SUBLANES * k, V7X_SUBLANES), :][None]

    def body(c, carry):
        t0 = pl.multiple_of(c * chunk, chunk)
        blk = pad_ref[pl.ds(t0, chunk + 2 * SEQ_PAD), :]
        acc = jnp.broadcast_to(tap(CONV_KSIZE), (groups, V7X_SUBLANES, n_ch))
        for res in range(V7X_SUBLANES):
            shifted = None
            for k in range(CONV_KSIZE):
                off = SEQ_PAD - half + k
                if off % V7X_SUBLANES != res:
                    continue
                if shifted is None:
                    shifted = blk[res:res + span, :]
                base = off - res
                acc = acc + shifted[base:base + chunk, :].reshape(groups, V7X_SUBLANES, n_ch) * tap(k)
        o_ref[pl.ds(t0, chunk), :] = acc.reshape(chunk, n_ch)
        return carry

    lax.fori_loop(0, seq // chunk, body, 0)


def _dwconv(u, a_col, gate_col, dw_w, dw_b, batch, seq):
    ksz, c = dw_w.shape
    tc = V7X_LANES
    chunk = min(128, seq)
    return pl.pallas_call(
        functools.partial(_dwconv_kernel, seq=seq, chunk=chunk),
        out_shape=jax.ShapeDtypeStruct((batch * seq, c), _F32),
        grid=(batch, c // tc),
        in_specs=[
            pl.BlockSpec((seq, tc), lambda b, j: (b, a_col // tc + j)),
            pl.BlockSpec((seq, tc), lambda b, j: (b, gate_col // tc + j)),
            pl.BlockSpec((ksz, tc), lambda b, j: (0, j)),
            pl.BlockSpec((1, tc), lambda b, j: (0, j)),
        ],
        out_specs=pl.BlockSpec((seq, tc), lambda b, j: (b, j)),
        scratch_shapes=[pltpu.VMEM((seq + 2 * SEQ_PAD, tc), _F32),
                        pltpu.VMEM((V7X_SUBLANES * (ksz + 1), tc), _F32)],
        compiler_params=_params("parallel", "parallel"),
        name="dwconv",
    )(u, u, dw_w, dw_b.reshape(1, c))


def _pw_kernel(h_ref, g_ref, w_ref, o_ref):
    x = h_ref[...]
    y = x * lax.rsqrt(jnp.mean(x * x, axis=-1, keepdims=True) + EPS) * g_ref[...]
    y = y * jax.nn.sigmoid(y)
    o_ref[...] = _dot(y.astype(_BF16), w_ref[...]).astype(o_ref.dtype)


def _pw(h, g, w, tm):
    m, c = h.shape
    return pl.pallas_call(
        _pw_kernel,
        out_shape=jax.ShapeDtypeStruct((m, c), _BF16),
        grid=(m // tm,),
        in_specs=[
            pl.BlockSpec((tm, c), lambda i: (i, 0)),
            pl.BlockSpec((1, c), lambda i: (0, 0)),
            pl.BlockSpec((c, c), lambda i: (0, 0)),
        ],
        out_specs=pl.BlockSpec((tm, c), lambda i: (i, 0)),
        compiler_params=_params("parallel"),
        name="conv_pw",
    )(h, g.reshape(1, c), w)


def _outproj_kernel(yp_ref, yc_ref, ya_ref, w_ref, x_ref, gate_ref, o_ref):
    kp = yp_ref.shape[1]
    kc = yc_ref.shape[1]
    yp, yc, ya = yp_ref[...], yc_ref[...], ya_ref[...]
    for sl in _col_chunks(o_ref.shape[1]):
        acc = _dot(yp, w_ref[pl.ds(0, kp), sl])
        acc = acc + _dot(yc, w_ref[pl.ds(kp, kc), sl])
        acc = acc + _dot(ya, w_ref[pl.ds(kp + kc, ya.shape[1]), sl])
        o_ref[:, sl] = x_ref[:, sl] + gate_ref[:, sl] * acc


def _outproj(yp, yc, ya, w, x, mod, gate_idx, rows, tm, tn, sides=()):
    m, d = x.shape
    tn = _tile(d, tn)
    lhs = lambda y: pl.BlockSpec((tm, y.shape[1]), lambda i, j: (i, 0))
    return _call_with_side_casts(
        _outproj_kernel, (yp, yc, ya, w, x, mod), sides,
        out_shape=jax.ShapeDtypeStruct((m, d), _F32),
        grid=(m // tm, d // tn),
        in_specs=[
            lhs(yp), lhs(yc), lhs(ya),
            pl.BlockSpec((w.shape[0], tn), lambda i, j: (0, j)),
            pl.BlockSpec((tm, tn), lambda i, j: (i, j)),
            _mod_spec(tn, gate_idx, rows, tm),
        ],
        out_specs=pl.BlockSpec((tm, tn), lambda i, j: (i, j)),
        name="outproj",
    )


def _mlp1_kernel(h_ref, w_ref, o_ref):
    h = h_ref[...]
    for sl in _col_chunks(o_ref.shape[1]):
        a = jnp.maximum(_dot(h, w_ref[:, sl]), 0.0)
        o_ref[:, sl] = (a * a).astype(o_ref.dtype)


def _mlp1(h, w, tm, tn, sides=()):
    m, d = h.shape
    f = w.shape[1]
    tn = _tile(f, tn)
    return _call_with_side_casts(
        _mlp1_kernel, (h, w), sides,
        out_shape=jax.ShapeDtypeStruct((m, f), _BF16),
        grid=(m // tm, f // tn),
        in_specs=[
            pl.BlockSpec((tm, d), lambda i, j: (i, 0)),
            pl.BlockSpec((d, tn), lambda i, j: (0, j)),
        ],
        out_specs=pl.BlockSpec((tm, tn), lambda i, j: (i, j)),
        name="mlp1",
    )


def _mlp2_kernel(a_ref, w_ref, x_ref, gate_ref, o_ref, *, nk):
    k = pl.program_id(2)
    chunks = _col_chunks(o_ref.shape[1])

    @pl.when(k == 0)
    def _():
        a = a_ref[...]
        for sl in chunks:
            o_ref[:, sl] = _dot(a, w_ref[:, sl])

    @pl.when(k > 0)
    def _():
        a = a_ref[...]
        for sl in chunks:
            o_ref[:, sl] += _dot(a, w_ref[:, sl])

    @pl.when(k == nk - 1)
    def _():
        o_ref[...] = x_ref[...] + gate_ref[...] * o_ref[...]


def _mlp2(a, w, x, mod, gate_idx, rows, tm, tn, tk):
    m, d = x.shape
    f = a.shape[1]
    tn = _tile(d, tn)
    tk = _tile(f, tk)
    nk = f // tk
    return pl.pallas_call(
        functools.partial(_mlp2_kernel, nk=nk),
        out_shape=jax.ShapeDtypeStruct((m, d), _F32),
        grid=(m // tm, d // tn, nk),
        in_specs=[
            pl.BlockSpec((tm, tk), lambda i, j, k: (i, k)),
            pl.BlockSpec((tk, tn), lambda i, j, k: (k, j)),
            pl.BlockSpec((tm, tn), lambda i, j, k: (i, j), pipeline_mode=pl.Buffered(1)),
            _mod_spec(tn, gate_idx, rows, tm),
        ],
        out_specs=pl.BlockSpec((tm, tn), lambda i, j, k: (i, j)),
        compiler_params=_params("parallel", "parallel", "arbitrary"),
        name="mlp2",
    )(a, w, x, mod)


def _mixers(u_pc, y_attn, p, batch, seq, tm):
    pool_width = p["pool_w"].shape[0] * p["pool_w"].shape[1]
    conv_width = p["conv_dw_w"].shape[1]
    y_pool = _pool(u_pc, p["pool_w"], p["pool_scale"], batch, seq)
    hconv = _dwconv(u_pc, pool_width, pool_width + conv_width, p["conv_dw_w"], p["conv_dw_b"], batch, seq)
    y_conv = _pw(hconv, p["conv_norm_g"], p["conv_pw_w"], tm)
    return y_pool, y_conv, y_attn


def _post_attention(x, ys, p, mod, rows, tm, raw=None, layer=None, next_w_in=None):
    p = dict(p)
    x, cast = _outproj(*ys, p["w_out"], x, mod, 2, rows, tm, 512,
                       sides=[_SideCast(raw["w_mlp1"], layer)] if raw else ())
    if raw:
        p["w_mlp1"] = cast[0]
    h = _normmod(x, p["norm2_g"], mod, 3, rows, min(tm, 256))
    sides = []
    if raw:
        sides = [_SideCast(raw["w_mlp2"], layer)] + ([_SideCast(raw["w_in"], layer + 1)] if next_w_in else [])
    a, cast = _mlp1(h, p["w_mlp1"], tm, 1024, sides=sides)
    if raw:
        p["w_mlp2"] = cast[0]
    w_in_next = cast[1] if raw and next_w_in else None
    return _mlp2(a, p["w_mlp2"], x, mod, 5, rows, tm, 1024, 4096), p, w_in_next


def kernel(x, c, ctx, c_ctx, w_ada, b_ada, norm1_g, norm2_g, w_in, pool_w, pool_scale, conv_dw_w, conv_dw_b,
           conv_norm_g, conv_pw_w, q_norm_g, k_norm_g, rpb, w_out, w_mlp1, w_mlp2):
    batch, seq, d = x.shape
    n_ctx = ctx.shape[1]
    depth = w_in.shape[0]
    n_heads = rpb.shape[1]
    attn_w = n_heads * HEAD_DIM
    pool_width = pool_w.shape[1] * pool_w.shape[2]
    conv_width = conv_dw_w.shape[2]
    off_q = pool_width + 2 * conv_width
    off_k = off_q + attn_w
    kh = min(WIN_H, seq // GRID_W)

    m_lat, m_ctx = batch * seq, batch * n_ctx
    tm_lat, tm_ctx = _tile(seq, 1024), _tile(m_ctx, 1024)
    ctx_row = batch
    lat_row = _Rows(base=0, tokens=seq)
    cx_row = _Rows(base=ctx_row, tokens=m_ctx)

    x = x.reshape(m_lat, d)
    xc = ctx.reshape(m_ctx, d)
    n_rows = -(-(batch + 1) // V7X_SUBLANES) * V7X_SUBLANES
    cvec = jnp.zeros((n_rows, d), _F32).at[:batch].set(c).at[ctx_row].set(c_ctx)
    qk_gain = jnp.concatenate([jnp.tile(q_norm_g, (1, n_heads)), jnp.tile(k_norm_g, (1, n_heads))], axis=1)

    raw = {"w_in": w_in, "w_mlp1": w_mlp1, "w_mlp2": w_mlp2}
    w_in_bf = _to_bf16(w_in, 0)
    for layer in range(depth):
        p = {
            "w_in": w_in_bf,
            "pool_w": _to_bf16(pool_w.reshape(depth, -1, pool_w.shape[-1]), layer).reshape(pool_w.shape[1:]),
            "pool_scale": pool_scale[layer],
            "conv_dw_w": conv_dw_w[layer], "conv_dw_b": conv_dw_b[layer],
            "conv_norm_g": conv_norm_g[layer], "conv_pw_w": _to_bf16(conv_pw_w, layer),
            "norm2_g": norm2_g[layer],
        }
        gains = qk_gain[layer:layer + 1]
        update_ctx = layer < depth - 1
        mod = _ada(cvec, w_ada, b_ada, layer).reshape(n_rows, N_MOD, 1, d)
        bias = _bias_table(rpb[layer])

        hc = _normmod(xc, norm1_g[layer], mod, 0, cx_row, min(tm_ctx, 256))
        if update_ctx:
            uc_pc, _ = _proj(hc, p["w_in"], 0, off_q, None, _F32, tm_ctx, 1024, "proj_pc")
            qkv_c, _ = _proj(hc, p["w_in"], off_q, 3 * attn_w, gains, _BF16, tm_ctx, 1024, "proj_qkv")
            kc_col, vc_col = attn_w, 2 * attn_w
        else:
            qkv_c, _ = _proj(hc, p["w_in"], off_k, 2 * attn_w, gains[:, attn_w:], _BF16, tm_ctx, 1024, "proj_kv")
            kc_col, vc_col = 0, attn_w

        h = _normmod(x, norm1_g[layer], mod, 0, lat_row, min(tm_lat, 256))
        u_pc, _ = _proj(h, p["w_in"], 0, off_q, None, _F32, tm_lat, 1024, "proj_pc")
        qkv, (p["w_out"],) = _proj(h, p["w_in"], off_q, 3 * attn_w, gains, _BF16, tm_lat, 1024, "proj_qkv",
                                   sides=[_SideCast(w_out, layer)])
        y_attn = _nattn(qkv, qkv_c, kc_col, vc_col, bias, batch, seq, n_ctx, n_heads)
        x, p, w_in_bf = _post_attention(x, _mixers(u_pc, y_attn, p, batch, seq, tm_lat), p, mod, lat_row, tm_lat,
                                        raw=raw, layer=layer, next_w_in=layer + 1 < depth)

        if update_ctx:
            yc_attn = _cattn(qkv_c, batch, n_ctx, n_heads)
            xc, _, _ = _post_attention(xc, _mixers(uc_pc, yc_attn, p, batch, n_ctx, tm_ctx), p, mod, cx_row, tm_ctx)

    return x.reshape(batch, seq, d)
```

```python
import functools
import math
from typing import NamedTuple

import jax
import jax.numpy as jnp
from jax import lax
from jax.experimental import pallas as pl
from jax.experimental.pallas import tpu as pltpu

GRID_W = 64
WIN_H = 8
WIN_W = 16
HEAD_DIM = 128
POOL_WINDOWS = (2, 4, 8, 16)
CONV_KSIZE = 31
N_MOD = 6
EPS = 1e-6
MASK_VALUE = -1e30

V7X_LANES = 128
V7X_SUBLANES = 8
V7X_MXU_COLS = 256
V7X_VMEM_LIMIT_BYTES = 56 * 1024 * 1024
V7X_VMEM_LIMIT_BIG_TILES_BYTES = 60 * 1024 * 1024
NATTN_PAIRS_PER_STEP = 4
SEQ_PAD = 16

_BF16 = jnp.bfloat16
_F32 = jnp.float32


def _params(*sem):
    return pltpu.CompilerParams(dimension_semantics=sem, vmem_limit_bytes=V7X_VMEM_LIMIT_BYTES)


def _tile(n, pref, *also):
    if n <= pref and not any(a % n for a in also):
        return n
    t = min(pref, n) - min(pref, n) % V7X_LANES
    while n % t or any(a % t for a in also):
        t -= V7X_LANES
    return t


def _dot(a, b):
    return jnp.dot(a, b, preferred_element_type=_F32)


def _cast_kernel(w_ref, o_ref):
    o_ref[...] = w_ref[...].astype(o_ref.dtype)


def _to_bf16(w, layer):
    _, r, c = w.shape
    tr, tc = _tile(r, 512), _tile(c, 2048)
    return pl.pallas_call(
        _cast_kernel,
        out_shape=jax.ShapeDtypeStruct((r, c), _BF16),
        grid=(r // tr, c // tc),
        in_specs=[pl.BlockSpec((None, tr, tc), lambda i, j: (layer, i, j))],
        out_specs=pl.BlockSpec((tr, tc), lambda i, j: (i, j)),
        compiler_params=_params("parallel", "parallel"),
        name="cast_bf16",
    )(w)


class _SideCast(NamedTuple):
    w: jax.Array
    layer: int


def _side_cast_plan(side, grid):
    _, r, c = side.w.shape
    bf16_rows = 2 * V7X_SUBLANES
    nb = max(b for b in range(1, grid[0] * grid[1] + 1) if r % b == 0 and (r // b) % bf16_rows == 0)
    tr = r // nb

    def blk(i, j):
        return jnp.minimum(i * grid[1] + j, nb - 1)

    return (nb,
            pl.BlockSpec((None, tr, c), lambda i, j, *_: (side.layer, blk(i, j), 0)),
            pl.BlockSpec((tr, c), lambda i, j, *_: (blk(i, j), 0)),
            jax.ShapeDtypeStruct((r, c), _BF16))


def _call_with_side_casts(body, args, sides, *, out_shape, grid, in_specs, out_specs, name):
    plans = [_side_cast_plan(s, grid) for s in sides]
    n_in, n_side = len(args), len(sides)

    def wrapped(*refs):
        body(*refs[:n_in], refs[n_in + n_side])
        step = pl.program_id(0) * grid[1] + pl.program_id(1)
        for (nb, _, _, _), src, dst in zip(plans, refs[n_in:n_in + n_side], refs[n_in + n_side + 1:]):
            @pl.when(step < nb)
            def _(src=src, dst=dst):
                dst[...] = src[...].astype(dst.dtype)

    outs = pl.pallas_call(
        wrapped,
        out_shape=[out_shape] + [p[3] for p in plans],
        grid=grid,
        in_specs=list(in_specs) + [p[1] for p in plans],
        out_specs=[out_specs] + [p[2] for p in plans],
        compiler_params=_params(*("arbitrary" if sides else "parallel",) * len(grid)),
        name=name,
    )(*args, *[s.w for s in sides])
    return outs[0], list(outs[1:])


def _ada_kernel(c_ref, w_ref, b_ref, o_ref):
    c = c_ref[...]
    s = (c * jax.nn.sigmoid(c)).astype(_BF16)
    o_ref[...] = _dot(s, w_ref[...].astype(_BF16)) + b_ref[...]


def _ada(cvec, w_ada, b_ada, layer):
    r, d = cvec.shape
    n = w_ada.shape[2]
    tn = _tile(n, 1024)
    return pl.pallas_call(
        _ada_kernel,
        out_shape=jax.ShapeDtypeStruct((r, n), _F32),
        grid=(n // tn,),
        in_specs=[
            pl.BlockSpec((r, d), lambda j: (0, 0)),
            pl.BlockSpec((None, d, tn), lambda j: (layer, 0, j)),
            pl.BlockSpec((None, 1, tn), lambda j: (layer, 0, j)),
        ],
        out_specs=pl.BlockSpec((r, tn), lambda j: (0, j)),
        compiler_params=_params("parallel"),
        name="ada",
    )(cvec, w_ada, b_ada.reshape(b_ada.shape[0], 1, n))


class _Rows(NamedTuple):
    base: int
    tokens: int

    def of_block(self, i, tm):
        return self.base + (i * tm) // self.tokens


def _mod_spec(tn, which, rows, tm):
    return pl.BlockSpec((None, None, 1, tn), lambda i, j=0, *_: (rows.of_block(i, tm), which, 0, j))


def _normmod_kernel(x_ref, g_ref, shift_ref, scale_ref, o_ref):
    x = x_ref[...]
    y = x * lax.rsqrt(jnp.mean(x * x, axis=-1, keepdims=True) + EPS) * g_ref[...]
    o_ref[...] = (y * (1.0 + scale_ref[...]) + shift_ref[...]).astype(o_ref.dtype)


def _normmod(x, g, mod, shift_idx, rows, tm):
    m, d = x.shape
    return pl.pallas_call(
        _normmod_kernel,
        out_shape=jax.ShapeDtypeStruct((m, d), _BF16),
        grid=(m // tm,),
        in_specs=[
            pl.BlockSpec((tm, d), lambda i: (i, 0)),
            pl.BlockSpec((1, d), lambda i: (0, 0)),
            _mod_spec(d, shift_idx, rows, tm),
            _mod_spec(d, shift_idx + 1, rows, tm),
        ],
        out_specs=pl.BlockSpec((tm, d), lambda i: (i, 0)),
        compiler_params=_params("parallel"),
        name="normmod",
    )(x, g.reshape(1, d), mod, mod)


def _col_chunks(n):
    c = V7X_MXU_COLS if n % V7X_MXU_COLS == 0 else n
    return [slice(s, s + c) for s in range(0, n, c)]


def _proj_plain_kernel(h_ref, w_ref, o_ref):
    h = h_ref[...]
    for sl in _col_chunks(o_ref.shape[1]):
        o_ref[:, sl] = _dot(h, w_ref[:, sl]).astype(o_ref.dtype)


def _proj_kernel(h_ref, w_ref, g_ref, o_ref, *, n_norm_blocks):
    normed = pl.program_id(1) < n_norm_blocks
    h = h_ref[...]
    for sl in _col_chunks(o_ref.shape[1]):
        acc = _dot(h, w_ref[:, sl])
        for c in range(0, acc.shape[1], HEAD_DIM):
            a = acc[:, c:c + HEAD_DIM]
            rstd = lax.rsqrt(jnp.mean(a * a, axis=-1, keepdims=True) + EPS)
            gain = g_ref[:, sl.start + c:sl.start + c + HEAD_DIM]
            y = a * jnp.where(normed, rstd, 1.0) * jnp.where(normed, gain, 1.0)
            o_ref[:, sl.start + c:sl.start + c + HEAD_DIM] = y.astype(o_ref.dtype)


def _proj(h, w, col0, ncols, gains, out_dtype, tm, tn, name, sides=()):
    m, d = h.shape
    tn = _tile(ncols, tn, col0, *(() if gains is None else (gains.shape[1],)))
    cb = col0 // tn
    in_specs = [
        pl.BlockSpec((tm, d), lambda i, j: (i, 0)),
        pl.BlockSpec((d, tn), lambda i, j: (0, cb + j)),
    ]
    if gains is None:
        body, args = _proj_plain_kernel, (h, w)
    else:
        n_norm_blocks = gains.shape[1] // tn
        body, args = functools.partial(_proj_kernel, n_norm_blocks=n_norm_blocks), (h, w, gains)
        in_specs.append(pl.BlockSpec((1, tn), lambda i, j: (0, jnp.minimum(j, n_norm_blocks - 1))))
    return _call_with_side_casts(
        body, args, sides,
        out_shape=jax.ShapeDtypeStruct((m, ncols), out_dtype),
        grid=(m // tm, ncols // tn),
        in_specs=in_specs,
        out_specs=pl.BlockSpec((tm, tn), lambda i, j: (i, j)),
        name=name,
    )


def _nattn_kernel(q_ref, k_ref, v_ref, kc_ref, vc_ref, b_ref, o_ref, v1_ref, vc1_ref, *, rows, kh):
    exp2_scale = HEAD_DIM ** -0.5 * math.log2(math.e)
    nu = min(kh + 1, rows)
    nt = (((1,), (1,)), ((), ()))
    tn = (((0,), (0,)), ((), ()))
    kc = kc_ref[...]
    second = lax.broadcasted_iota(jnp.int32, (1, 2 * GRID_W), 1) >= GRID_W
    for src, dst in ((v_ref, v1_ref), (vc_ref, vc1_ref)):
        dst[:, :HEAD_DIM] = src[...]
        dst[:, HEAD_DIM:] = jnp.ones(src.shape, dst.dtype)
    vc1 = vc1_ref[...]

    def scores(p):
        i = 2 * p
        ustart = jnp.clip(i - kh // 2, 0, rows - nu)
        start0 = jnp.clip(i - kh // 2, 0, rows - kh)
        start1 = jnp.clip(i + 1 - kh // 2, 0, rows - kh)
        lo = jnp.where(second, start1, start0) - ustart
        d0 = ustart - i + (WIN_H - 1)
        qsl = pl.ds(pl.multiple_of(i * GRID_W, 2 * GRID_W), 2 * GRID_W)
        win = pl.ds(pl.multiple_of(ustart * GRID_W, GRID_W), nu * GRID_W)
        q2 = q_ref[qsl, :]
        bias = jnp.concatenate(
            [jnp.where((lo <= r) & (r < lo + kh), b_ref[d0 + r], MASK_VALUE) for r in range(nu)], axis=0)
        t_nb = lax.dot_general(k_ref[win, :], q2, nt, preferred_element_type=_F32) + bias
        t_cx = lax.dot_general(kc, q2, nt, preferred_element_type=_F32)
        return qsl, win, t_nb, t_cx

    def weights(t_nb, t_cx):
        m = jnp.maximum(jnp.max(t_nb, axis=0, keepdims=True), jnp.max(t_cx, axis=0, keepdims=True))
        return (jnp.exp2((t_nb - m) * exp2_scale).astype(_BF16), jnp.exp2((t_cx - m) * exp2_scale).astype(_BF16))

    def step(t, carry):
        staged = [scores(NATTN_PAIRS_PER_STEP * t + g) for g in range(NATTN_PAIRS_PER_STEP)]
        probs = [weights(t_nb, t_cx) for _, _, t_nb, t_cx in staged]
        for (qsl, win, _, _), (e_nb, e_cx) in zip(staged, probs):
            r = lax.dot_general(e_nb, v1_ref[win, :], tn, preferred_element_type=_F32)
            r = r + lax.dot_general(e_cx, vc1, tn, preferred_element_type=_F32)
            o_ref[qsl, :] = (r[:, :HEAD_DIM] / r[:, HEAD_DIM:]).astype(o_ref.dtype)
        return carry

    lax.fori_loop(0, rows // (2 * NATTN_PAIRS_PER_STEP), step, 0)


def _bias_table(rpb):
    j = jnp.arange(GRID_W)
    col_start = jnp.clip(j - WIN_W // 2, 0, GRID_W - WIN_W)
    valid = (j[None, :] >= col_start[:, None]) & (j[None, :] < col_start[:, None] + WIN_W)
    dc = jnp.clip(j[None, :] - j[:, None] + (WIN_W - 1), 0, 2 * WIN_W - 2)
    onehot = (dc[None] == jnp.arange(2 * WIN_W - 1)[:, None, None]).astype(_F32)
    t = jnp.einsum("hdc,cqk->hdqk", rpb, onehot, precision=lax.Precision.HIGHEST)
    t = jnp.where(valid[None, None], t * HEAD_DIM ** 0.5, MASK_VALUE)
    t = jnp.swapaxes(t, 2, 3)
    masked = jnp.full_like(t[:, :1], MASK_VALUE)
    first = jnp.concatenate([t, masked], axis=1)
    second = jnp.concatenate([masked, t], axis=1)
    return jnp.concatenate([first, second], axis=-1).astype(_F32)


def _nattn(qkv, kvc, kc_col, vc_col, bias, batch, seq, n_ctx, n_heads):
    rows = seq // GRID_W
    kh = min(WIN_H, rows)
    assert rows % (2 * NATTN_PAIRS_PER_STEP) == 0
    a = n_heads * HEAD_DIM
    hb = lambda off: off // HEAD_DIM
    return pl.pallas_call(
        functools.partial(_nattn_kernel, rows=rows, kh=kh),
        out_shape=jax.ShapeDtypeStruct((batch * seq, a), _BF16),
        grid=(batch, n_heads),
        in_specs=[
            pl.BlockSpec((seq, HEAD_DIM), lambda b, h: (b, h)),
            pl.BlockSpec((seq, HEAD_DIM), lambda b, h: (b, hb(a) + h)),
            pl.BlockSpec((seq, HEAD_DIM), lambda b, h: (b, hb(2 * a) + h)),
            pl.BlockSpec((n_ctx, HEAD_DIM), lambda b, h: (b, hb(kc_col) + h)),
            pl.BlockSpec((n_ctx, HEAD_DIM), lambda b, h: (b, hb(vc_col) + h)),
            pl.BlockSpec((None, 2 * WIN_H, GRID_W, 2 * GRID_W), lambda b, h: (h, 0, 0, 0)),
        ],
        out_specs=pl.BlockSpec((seq, HEAD_DIM), lambda b, h: (b, h)),
        scratch_shapes=[pltpu.VMEM((seq, 2 * HEAD_DIM), _BF16), pltpu.VMEM((n_ctx, 2 * HEAD_DIM), _BF16)],
        compiler_params=_params("parallel", "parallel"),
        name="nattn",
    )(qkv, qkv, qkv, kvc, kvc, bias)


def _cattn_kernel(q_ref, k_ref, v_ref, o_ref):
    scale = HEAD_DIM ** -0.5
    s = lax.dot_general(q_ref[...], k_ref[...], (((1,), (1,)), ((), ())), preferred_element_type=_F32) * scale
    e = jnp.exp(s - jnp.max(s, axis=-1, keepdims=True))
    p = e * (1.0 / jnp.sum(e, axis=-1, keepdims=True))
    o_ref[...] = _dot(p.astype(_BF16), v_ref[...]).astype(o_ref.dtype)


def _cattn(qkv, batch, n_ctx, n_heads):
    a = n_heads * HEAD_DIM
    spec = lambda off: pl.BlockSpec((n_ctx, HEAD_DIM), lambda b, h: (b, off // HEAD_DIM + h))
    return pl.pallas_call(
        _cattn_kernel,
        out_shape=jax.ShapeDtypeStruct((batch * n_ctx, a), _BF16),
        grid=(batch, n_heads),
        in_specs=[spec(0), spec(a), spec(2 * a)],
        out_specs=spec(0),
        compiler_params=_params("parallel", "parallel"),
        name="cattn",
    )(qkv, qkv, qkv)


def _fill_padded(pad_ref, seq, fill_chunk, chunk):
    zeros = jnp.zeros((SEQ_PAD, pad_ref.shape[1]), pad_ref.dtype)
    pad_ref[pl.ds(0, SEQ_PAD), :] = zeros
    pad_ref[pl.ds(SEQ_PAD + seq, SEQ_PAD), :] = zeros

    def body(c, carry):
        t0 = pl.multiple_of(c * chunk, chunk)
        pad_ref[pl.ds(SEQ_PAD + t0, chunk), :] = fill_chunk(t0)
        return carry

    lax.fori_loop(0, seq // chunk, body, 0)


def _pool_kernel(u_ref, w_ref, s_ref, o_ref, pad_ref, *, seq, chunk):
    g = pl.program_id(1)
    _fill_padded(pad_ref, seq, lambda t0: u_ref[pl.ds(t0, chunk), :], chunk)
    w = w_ref[...]
    s = s_ref[...]

    for gi, win in enumerate(POOL_WINDOWS):

        @pl.when(g == gi)
        def _(win=win):
            def body(c, carry):
                t0 = pl.multiple_of(c * chunk, chunk)
                blk = pad_ref[pl.ds(t0, chunk + 2 * SEQ_PAD), :]
                tot = None
                for sft in range(-(win // 2), win - win // 2):
                    part = blk[SEQ_PAD + sft:SEQ_PAD + sft + chunk, :]
                    tot = part if tot is None else tot + part
                t = t0 + lax.broadcasted_iota(jnp.int32, (chunk, blk.shape[1]), 0)
                lo = jnp.maximum(t - win // 2, 0)
                hi = jnp.minimum(t + (win - 1 - win // 2), seq - 1)
                cnt = (hi - lo + 1).astype(_F32)
                pooled = tot / cnt - blk[SEQ_PAD:SEQ_PAD + chunk, :]
                o_ref[pl.ds(t0, chunk), :] = (_dot(pooled.astype(_BF16), w) * s).astype(o_ref.dtype)
                return carry

            lax.fori_loop(0, seq // chunk, body, 0)


def _pool(u, pool_w, pool_scale, batch, seq):
    n_groups, c, _ = pool_w.shape
    chunk = min(128, seq)
    return pl.pallas_call(
        functools.partial(_pool_kernel, seq=seq, chunk=chunk),
        out_shape=jax.ShapeDtypeStruct((batch * seq, n_groups * c), _BF16),
        grid=(batch, n_groups),
        in_specs=[
            pl.BlockSpec((seq, c), lambda b, g: (b, g)),
            pl.BlockSpec((None, c, c), lambda b, g: (g, 0, 0)),
            pl.BlockSpec((1, c), lambda b, g: (0, g)),
        ],
        out_specs=pl.BlockSpec((seq, c), lambda b, g: (b, g)),
        scratch_shapes=[pltpu.VMEM((seq + 2 * SEQ_PAD, c), _F32)],
        compiler_params=_params("parallel", "parallel"),
        name="pool",
    )(u, pool_w, pool_scale.reshape(1, n_groups * c))


def _dwconv_kernel(a_ref, gate_ref, w_ref, b_ref, o_ref, pad_ref, wb_ref, *, seq, chunk):
    def glu(t0):
        return a_ref[pl.ds(t0, chunk), :] * jax.nn.sigmoid(gate_ref[pl.ds(t0, chunk), :])

    _fill_padded(pad_ref, seq, glu, chunk)
    n_ch = o_ref.shape[1]
    for k in range(CONV_KSIZE):
        wb_ref[pl.ds(V7X_SUBLANES * k, V7X_SUBLANES), :] = jnp.broadcast_to(w_ref[k:k + 1, :], (V7X_SUBLANES, n_ch))
    wb_ref[pl.ds(V7X_SUBLANES * CONV_KSIZE, V7X_SUBLANES), :] = jnp.broadcast_to(b_ref[...], (V7X_SUBLANES, n_ch))
    half = CONV_KSIZE // 2
    span = chunk + 2 * SEQ_PAD - V7X_SUBLANES
    groups = chunk // V7X_SUBLANES

    def tap(k):
        return wb_ref[pl.ds(V7X_SUBLANES * k, V7X_SUBLANES), :][None]

    def body(c, carry):
        t0 = pl.multiple_of(c * chunk, chunk)
        blk = pad_ref[pl.ds(t0, chunk + 2 * SEQ_PAD), :]
        acc = jnp.broadcast_to(tap(CONV_KSIZE), (groups, V7X_SUBLANES, n_ch))
        for res in range(V7X_SUBLANES):
            shifted = None
            for k in range(CONV_KSIZE):
                off = SEQ_PAD - half + k
                if off % V7X_SUBLANES != res:
                    continue
                if shifted is None:
                    shifted = blk[res:res + span, :]
                base = off - res
                acc = acc + shifted[base:base + chunk, :].reshape(groups, V7X_SUBLANES, n_ch) * tap(k)
        o_ref[pl.ds(t0, chunk), :] = acc.reshape(chunk, n_ch)
        return carry

    lax.fori_loop(0, seq // chunk, body, 0)


def _dwconv(u, a_col, gate_col, dw_w, dw_b, batch, seq):
    ksz, c = dw_w.shape
    tc = V7X_LANES
    chunk = min(128, seq)
    return pl.pallas_call(
        functools.partial(_dwconv_kernel, seq=seq, chunk=chunk),
        out_shape=jax.ShapeDtypeStruct((batch * seq, c), _F32),
        grid=(batch, c // tc),
        in_specs=[
            pl.BlockSpec((seq, tc), lambda b, j: (b, a_col // tc + j)),
            pl.BlockSpec((seq, tc), lambda b, j: (b, gate_col // tc + j)),
            pl.BlockSpec((ksz, tc), lambda b, j: (0, j)),
            pl.BlockSpec((1, tc), lambda b, j: (0, j)),
        ],
        out_specs=pl.BlockSpec((seq, tc), lambda b, j: (b, j)),
        scratch_shapes=[pltpu.VMEM((seq + 2 * SEQ_PAD, tc), _F32),
                        pltpu.VMEM((V7X_SUBLANES * (ksz + 1), tc), _F32)],
        compiler_params=_params("parallel", "parallel"),
        name="dwconv",
    )(u, u, dw_w, dw_b.reshape(1, c))


def _pw_kernel(h_ref, g_ref, w_ref, o_ref):
    x = h_ref[...]
    y = x * lax.rsqrt(jnp.mean(x * x, axis=-1, keepdims=True) + EPS) * g_ref[...]
    y = y * jax.nn.sigmoid(y)
    o_ref[...] = _dot(y.astype(_BF16), w_ref[...]).astype(o_ref.dtype)


def _pw(h, g, w, tm):
    m, c = h.shape
    return pl.pallas_call(
        _pw_kernel,
        out_shape=jax.ShapeDtypeStruct((m, c), _BF16),
        grid=(m // tm,),
        in_specs=[
            pl.BlockSpec((tm, c), lambda i: (i, 0)),
            pl.BlockSpec((1, c), lambda i: (0, 0)),
            pl.BlockSpec((c, c), lambda i: (0, 0)),
        ],
        out_specs=pl.BlockSpec((tm, c), lambda i: (i, 0)),
        compiler_params=_params("parallel"),
        name="conv_pw",
    )(h, g.reshape(1, c), w)


def _outproj_kernel(yp_ref, yc_ref, ya_ref, w_ref, x_ref, gate_ref, o_ref):
    kp = yp_ref.shape[1]
    kc = yc_ref.shape[1]
    yp, yc, ya = yp_ref[...], yc_ref[...], ya_ref[...]
    for sl in _col_chunks(o_ref.shape[1]):
        acc = _dot(yp, w_ref[pl.ds(0, kp), sl])
        acc = acc + _dot(yc, w_ref[pl.ds(kp, kc), sl])
        acc = acc + _dot(ya, w_ref[pl.ds(kp + kc, ya.shape[1]), sl])
        o_ref[:, sl] = x_ref[:, sl] + gate_ref[:, sl] * acc


def _outproj(yp, yc, ya, w, x, mod, gate_idx, rows, tm, tn, sides=()):
    m, d = x.shape
    tn = _tile(d, tn)
    lhs = lambda y: pl.BlockSpec((tm, y.shape[1]), lambda i, j: (i, 0))
    return _call_with_side_casts(
        _outproj_kernel, (yp, yc, ya, w, x, mod), sides,
        out_shape=jax.ShapeDtypeStruct((m, d), _F32),
        grid=(m // tm, d // tn),
        in_specs=[
            lhs(yp), lhs(yc), lhs(ya),
            pl.BlockSpec((w.shape[0], tn), lambda i, j: (0, j)),
            pl.BlockSpec((tm, tn), lambda i, j: (i, j)),
            _mod_spec(tn, gate_idx, rows, tm),
        ],
        out_specs=pl.BlockSpec((tm, tn), lambda i, j: (i, j)),
        name="outproj",
    )


def _mlp1_kernel(h_ref, w_ref, o_ref):
    h = h_ref[...]
    for sl in _col_chunks(o_ref.shape[1]):
        a = jnp.maximum(_dot(h, w_ref[:, sl]), 0.0)
        o_ref[:, sl] = (a * a).astype(o_ref.dtype)


def _mlp1(h, w, tm, tn, sides=()):
    m, d = h.shape
    f = w.shape[1]
    tn = _tile(f, tn)
    return _call_with_side_casts(
        _mlp1_kernel, (h, w), sides,
        out_shape=jax.ShapeDtypeStruct((m, f), _BF16),
        grid=(m // tm, f // tn),
        in_specs=[
            pl.BlockSpec((tm, d), lambda i, j: (i, 0)),
            pl.BlockSpec((d, tn), lambda i, j: (0, j)),
        ],
        out_specs=pl.BlockSpec((tm, tn), lambda i, j: (i, j)),
        name="mlp1",
    )


def _mlp2_kernel(a_ref, w_ref, x_ref, gate_ref, o_ref, *, nk):
    k = pl.program_id(2)
    chunks = _col_chunks(o_ref.shape[1])

    @pl.when(k == 0)
    def _():
        for sl in chunks:
            o_ref[:, sl] = _dot(a_ref[...], w_ref[:, sl])

    @pl.when(k > 0)
    def _():
        for sl in chunks:
            o_ref[:, sl] += _dot(a_ref[...], w_ref[:, sl])

    @pl.when(k == nk - 1)
    def _():
        o_ref[...] = x_ref[...] + gate_ref[...] * o_ref[...]


def _mlp2(a, w, x, mod, gate_idx, rows, tm, tn, tk):
    m, d = x.shape
    f = a.shape[1]
    tn = _tile(d, tn)
    tk = _tile(f, tk)
    nk = f // tk
    return pl.pallas_call(
        functools.partial(_mlp2_kernel, nk=nk),
        out_shape=jax.ShapeDtypeStruct((m, d), _F32),
        grid=(m // tm, d // tn, nk),
        in_specs=[
            pl.BlockSpec((tm, tk), lambda i, j, k: (i, k)),
            pl.BlockSpec((tk, tn), lambda i, j, k: (k, j)),
            pl.BlockSpec((tm, tn), lambda i, j, k: (i, j)),
            _mod_spec(tn, gate_idx, rows, tm),
        ],
        out_specs=pl.BlockSpec((tm, tn), lambda i, j, k: (i, j)),
        compiler_params=pltpu.CompilerParams(dimension_semantics=("parallel", "parallel", "arbitrary"),
                                             vmem_limit_bytes=V7X_VMEM_LIMIT_BIG_TILES_BYTES),
        name="mlp2",
    )(a, w, x, mod)


def _mixers(u_pc, y_attn, p, batch, seq, tm):
    pool_width = p["pool_w"].shape[0] * p["pool_w"].shape[1]
    conv_width = p["conv_dw_w"].shape[1]
    y_pool = _pool(u_pc, p["pool_w"], p["pool_scale"], batch, seq)
    hconv = _dwconv(u_pc, pool_width, pool_width + conv_width, p["conv_dw_w"], p["conv_dw_b"], batch, seq)
    y_conv = _pw(hconv, p["conv_norm_g"], p["conv_pw_w"], tm)
    return y_pool, y_conv, y_attn


def _post_attention(x, ys, p, mod, rows, tm, raw=None, layer=None, next_w_in=None):
    p = dict(p)
    x, cast = _outproj(*ys, p["w_out"], x, mod, 2, rows, tm, 512,
                       sides=[_SideCast(raw["w_mlp1"], layer)] if raw else ())
    if raw:
        p["w_mlp1"] = cast[0]
    h = _normmod(x, p["norm2_g"], mod, 3, rows, min(tm, 256))
    sides = []
    if raw:
        sides = [_SideCast(raw["w_mlp2"], layer)] + ([_SideCast(raw["w_in"], layer + 1)] if next_w_in else [])
    a, cast = _mlp1(h, p["w_mlp1"], tm, 1024, sides=sides)
    if raw:
        p["w_mlp2"] = cast[0]
    w_in_next = cast[1] if raw and next_w_in else None
    return _mlp2(a, p["w_mlp2"], x, mod, 5, rows, tm, 1024, 4096), p, w_in_next


def kernel(x, c, ctx, c_ctx, w_ada, b_ada, norm1_g, norm2_g, w_in, pool_w, pool_scale, conv_dw_w, conv_dw_b,
           conv_norm_g, conv_pw_w, q_norm_g, k_norm_g, rpb, w_out, w_mlp1, w_mlp2):
    batch, seq, d = x.shape
    n_ctx = ctx.shape[1]
    depth = w_in.shape[0]
    n_heads = rpb.shape[1]
    attn_w = n_heads * HEAD_DIM
    pool_width = pool_w.shape[1] * pool_w.shape[2]
    conv_width = conv_dw_w.shape[2]
    off_q = pool_width + 2 * conv_width
    off_k = off_q + attn_w
    kh = min(WIN_H, seq // GRID_W)

    m_lat, m_ctx = batch * seq, batch * n_ctx
    tm_lat, tm_ctx = _tile(seq, 1024), _tile(m_ctx, 1024)
    ctx_row = batch
    lat_row = _Rows(base=0, tokens=seq)
    cx_row = _Rows(base=ctx_row, tokens=m_ctx)

    x = x.reshape(m_lat, d)
    xc = ctx.reshape(m_ctx, d)
    n_rows = -(-(batch + 1) // V7X_SUBLANES) * V7X_SUBLANES
    cvec = jnp.zeros((n_rows, d), _F32).at[:batch].set(c).at[ctx_row].set(c_ctx)
    qk_gain = jnp.concatenate([jnp.tile(q_norm_g, (1, n_heads)), jnp.tile(k_norm_g, (1, n_heads))], axis=1)

    raw = {"w_in": w_in, "w_mlp1": w_mlp1, "w_mlp2": w_mlp2}
    w_in_bf = _to_bf16(w_in, 0)
    for layer in range(depth):
        p = {
            "w_in": w_in_bf,
            "pool_w": _to_bf16(pool_w.reshape(depth, -1, pool_w.shape[-1]), layer).reshape(pool_w.shape[1:]),
            "pool_scale": pool_scale[layer],
            "conv_dw_w": conv_dw_w[layer], "conv_dw_b": conv_dw_b[layer],
            "conv_norm_g": conv_norm_g[layer], "conv_pw_w": _to_bf16(conv_pw_w, layer),
            "norm2_g": norm2_g[layer],
        }
        gains = qk_gain[layer:layer + 1]
        update_ctx = layer < depth - 1
        mod = _ada(cvec, w_ada, b_ada, layer).reshape(n_rows, N_MOD, 1, d)
        bias = _bias_table(rpb[layer])

        hc = _normmod(xc, norm1_g[layer], mod, 0, cx_row, min(tm_ctx, 256))
        if update_ctx:
            uc_pc, _ = _proj(hc, p["w_in"], 0, off_q, None, _F32, tm_ctx, 1024, "proj_pc")
            qkv_c, _ = _proj(hc, p["w_in"], off_q, 3 * attn_w, gains, _BF16, tm_ctx, 1024, "proj_qkv")
            kc_col, vc_col = attn_w, 2 * attn_w
        else:
            qkv_c, _ = _proj(hc, p["w_in"], off_k, 2 * attn_w, gains[:, attn_w:], _BF16, tm_ctx, 1024, "proj_kv")
            kc_col, vc_col = 0, attn_w

        h = _normmod(x, norm1_g[layer], mod, 0, lat_row, min(tm_lat, 256))
        u_pc, _ = _proj(h, p["w_in"], 0, off_q, None, _F32, tm_lat, 1024, "proj_pc")
        qkv, (p["w_out"],) = _proj(h, p["w_in"], off_q, 3 * attn_w, gains, _BF16, tm_lat, 1024, "proj_qkv",
                                   sides=[_SideCast(w_out, layer)])
        y_attn = _nattn(qkv, qkv_c, kc_col, vc_col, bias, batch, seq, n_ctx, n_heads)
        x, p, w_in_bf = _post_attention(x, _mixers(u_pc, y_attn, p, batch, seq, tm_lat), p, mod, lat_row, tm_lat,
                                        raw=raw, layer=layer, next_w_in=layer + 1 < depth)

        if update_ctx:
            yc_attn = _cattn(qkv_c, batch, n_ctx, n_heads)
            xc, _, _ = _post_attention(xc, _mixers(uc_pc, yc_attn, p, batch, n_ctx, tm_ctx), p, mod, cx_row, tm_ctx)

    return x.reshape(batch, seq, d)
```

```python
import functools
import math
from typing import NamedTuple

import jax
import jax.numpy as jnp
from jax import lax
from jax.experimental import pallas as pl
from jax.experimental.pallas import tpu as pltpu

GRID_W = 64
WIN_H = 8
WIN_W = 16
HEAD_DIM = 128
POOL_WINDOWS = (2, 4, 8, 16)
CONV_KSIZE = 31
N_MOD = 6
EPS = 1e-6
MASK_VALUE = -1e30

V7X_LANES = 128
V7X_SUBLANES = 8
V7X_MXU_COLS = 256
V7X_VMEM_LIMIT_BYTES = 56 * 1024 * 1024
V7X_VMEM_LIMIT_BIG_TILES_BYTES = 60 * 1024 * 1024
NATTN_PAIRS_PER_STEP = 4
NORMMOD_ROWS = 512
SEQ_PAD = 16

_BF16 = jnp.bfloat16
_F32 = jnp.float32


def _params(*sem):
    return pltpu.CompilerParams(dimension_semantics=sem, vmem_limit_bytes=V7X_VMEM_LIMIT_BYTES)


def _tile(n, pref, *also):
    if n <= pref and not any(a % n for a in also):
        return n
    t = min(pref, n) - min(pref, n) % V7X_LANES
    while n % t or any(a % t for a in also):
        t -= V7X_LANES
    return t


def _dot(a, b):
    return jnp.dot(a, b, preferred_element_type=_F32)


def _cast_kernel(w_ref, o_ref):
    o_ref[...] = w_ref[...].astype(o_ref.dtype)


def _to_bf16(w, layer):
    _, r, c = w.shape
    tr, tc = _tile(r, 512), _tile(c, 2048)
    return pl.pallas_call(
        _cast_kernel,
        out_shape=jax.ShapeDtypeStruct((r, c), _BF16),
        grid=(r // tr, c // tc),
        in_specs=[pl.BlockSpec((None, tr, tc), lambda i, j: (layer, i, j))],
        out_specs=pl.BlockSpec((tr, tc), lambda i, j: (i, j)),
        compiler_params=_params("parallel", "parallel"),
        name="cast_bf16",
    )(w)


class _SideCast(NamedTuple):
    w: jax.Array
    layer: int


def _side_cast_plan(side, grid):
    _, r, c = side.w.shape
    bf16_rows = 2 * V7X_SUBLANES
    nb = max(b for b in range(1, grid[0] * grid[1] + 1) if r % b == 0 and (r // b) % bf16_rows == 0)
    tr = r // nb

    def blk(i, j):
        return jnp.minimum(i * grid[1] + j, nb - 1)

    return (nb,
            pl.BlockSpec((None, tr, c), lambda i, j, *_: (side.layer, blk(i, j), 0)),
            pl.BlockSpec((tr, c), lambda i, j, *_: (blk(i, j), 0)),
            jax.ShapeDtypeStruct((r, c), _BF16))


def _call_with_side_casts(body, args, sides, *, out_shape, grid, in_specs, out_specs, name):
    plans = [_side_cast_plan(s, grid) for s in sides]
    n_in, n_side = len(args), len(sides)

    def wrapped(*refs):
        body(*refs[:n_in], refs[n_in + n_side])
        step = pl.program_id(0) * grid[1] + pl.program_id(1)
        for (nb, _, _, _), src, dst in zip(plans, refs[n_in:n_in + n_side], refs[n_in + n_side + 1:]):
            @pl.when(step < nb)
            def _(src=src, dst=dst):
                dst[...] = src[...].astype(dst.dtype)

    outs = pl.pallas_call(
        wrapped,
        out_shape=[out_shape] + [p[3] for p in plans],
        grid=grid,
        in_specs=list(in_specs) + [p[1] for p in plans],
        out_specs=[out_specs] + [p[2] for p in plans],
        compiler_params=_params(*("arbitrary" if sides else "parallel",) * len(grid)),
        name=name,
    )(*args, *[s.w for s in sides])
    return outs[0], list(outs[1:])


def _ada_kernel(c_ref, w_ref, b_ref, o_ref):
    c = c_ref[...]
    s = (c * jax.nn.sigmoid(c)).astype(_BF16)
    o_ref[...] = _dot(s, w_ref[...].astype(_BF16)) + b_ref[...]


def _ada(cvec, w_ada, b_ada, layer):
    r, d = cvec.shape
    n = w_ada.shape[2]
    tn = _tile(n, 1024)
    return pl.pallas_call(
        _ada_kernel,
        out_shape=jax.ShapeDtypeStruct((r, n), _F32),
        grid=(n // tn,),
        in_specs=[
            pl.BlockSpec((r, d), lambda j: (0, 0)),
            pl.BlockSpec((None, d, tn), lambda j: (layer, 0, j)),
            pl.BlockSpec((None, 1, tn), lambda j: (layer, 0, j)),
        ],
        out_specs=pl.BlockSpec((r, tn), lambda j: (0, j)),
        compiler_params=_params("parallel"),
        name="ada",
    )(cvec, w_ada, b_ada.reshape(b_ada.shape[0], 1, n))


class _Rows(NamedTuple):
    base: int
    tokens: int

    def of_block(self, i, tm):
        return self.base + (i * tm) // self.tokens


def _mod_spec(tn, which, rows, tm):
    return pl.BlockSpec((None, None, 1, tn), lambda i, j=0, *_: (rows.of_block(i, tm), which, 0, j))


def _normmod_kernel(x_ref, g_ref, shift_ref, scale_ref, o_ref):
    group = 2 * V7X_SUBLANES
    gain = g_ref[...]
    one_plus_scale = 1.0 + scale_ref[...]
    shift = shift_ref[...]

    def body(r, carry):
        rs = pl.ds(pl.multiple_of(r * group, group), group)
        x = x_ref[rs, :]
        y = x * lax.rsqrt(jnp.mean(x * x, axis=-1, keepdims=True) + EPS) * gain
        o_ref[rs, :] = (y * one_plus_scale + shift).astype(o_ref.dtype)
        return carry

    lax.fori_loop(0, x_ref.shape[0] // group, body, 0, unroll=4)


def _normmod(x, g, mod, shift_idx, rows):
    m, d = x.shape
    tm = _tile(rows.tokens, NORMMOD_ROWS)
    return pl.pallas_call(
        _normmod_kernel,
        out_shape=jax.ShapeDtypeStruct((m, d), _BF16),
        grid=(m // tm,),
        in_specs=[
            pl.BlockSpec((tm, d), lambda i: (i, 0)),
            pl.BlockSpec((1, d), lambda i: (0, 0)),
            _mod_spec(d, shift_idx, rows, tm),
            _mod_spec(d, shift_idx + 1, rows, tm),
        ],
        out_specs=pl.BlockSpec((tm, d), lambda i: (i, 0)),
        compiler_params=_params("parallel"),
        name="normmod",
    )(x, g.reshape(1, d), mod, mod)


def _col_chunks(n):
    c = V7X_MXU_COLS if n % V7X_MXU_COLS == 0 else n
    return [slice(s, s + c) for s in range(0, n, c)]


def _proj_plain_kernel(h_ref, w_ref, o_ref):
    h = h_ref[...]
    for sl in _col_chunks(o_ref.shape[1]):
        o_ref[:, sl] = _dot(h, w_ref[:, sl]).astype(o_ref.dtype)


def _proj_kernel(h_ref, w_ref, g_ref, o_ref, *, n_norm_blocks):
    normed = pl.program_id(1) < n_norm_blocks
    h = h_ref[...]
    for sl in _col_chunks(w_ref.shape[1]):
        acc = _dot(h, w_ref[:, sl])
        for c in range(0, acc.shape[1], HEAD_DIM):
            a = acc[:, c:c + HEAD_DIM]
            rstd = lax.rsqrt(jnp.mean(a * a, axis=-1, keepdims=True) + EPS)
            gain = g_ref[:, sl.start + c:sl.start + c + HEAD_DIM]
            y = a * jnp.where(normed, rstd, 1.0) * jnp.where(normed, gain, 1.0)
            o_ref[(sl.start + c) // HEAD_DIM] = y.astype(o_ref.dtype)


def _proj(h, w, col0, ncols, gains, out_dtype, tm, tn, name, sides=()):
    m, d = h.shape
    tn = _tile(ncols, tn, col0, *(() if gains is None else (gains.shape[1],)))
    cb = col0 // tn
    in_specs = [
        pl.BlockSpec((tm, d), lambda i, j: (i, 0)),
        pl.BlockSpec((d, tn), lambda i, j: (0, cb + j)),
    ]
    if gains is None:
        body, args = _proj_plain_kernel, (h, w)
        out_shape = jax.ShapeDtypeStruct((m, ncols), out_dtype)
        out_spec = pl.BlockSpec((tm, tn), lambda i, j: (i, j))
    else:
        n_norm_blocks = gains.shape[1] // tn
        body, args = functools.partial(_proj_kernel, n_norm_blocks=n_norm_blocks), (h, w, gains)
        in_specs.append(pl.BlockSpec((1, tn), lambda i, j: (0, jnp.minimum(j, n_norm_blocks - 1))))
        out_shape = jax.ShapeDtypeStruct((ncols // HEAD_DIM, m, HEAD_DIM), out_dtype)
        out_spec = pl.BlockSpec((tn // HEAD_DIM, tm, HEAD_DIM), lambda i, j: (j, i, 0))
    return _call_with_side_casts(
        body, args, sides,
        out_shape=out_shape,
        grid=(m // tm, ncols // tn),
        in_specs=in_specs,
        out_specs=out_spec,
        name=name,
    )


def _nattn_kernel(q_ref, k_ref, v_ref, kc_ref, vc_ref, b_ref, o_ref, v1_ref, vc1_ref, *, rows, kh):
    exp2_scale = HEAD_DIM ** -0.5 * math.log2(math.e)
    nu = min(kh + 1, rows)
    nt = (((1,), (1,)), ((), ()))
    tn = (((0,), (0,)), ((), ()))
    kc = kc_ref[...]
    second = lax.broadcasted_iota(jnp.int32, (1, 2 * GRID_W), 1) >= GRID_W
    for src, dst in ((v_ref, v1_ref), (vc_ref, vc1_ref)):
        dst[:, :HEAD_DIM] = src[...]
        dst[:, HEAD_DIM:] = jnp.ones(src.shape, dst.dtype)
    vc1 = vc1_ref[...]

    def scores(p):
        i = 2 * p
        ustart = jnp.clip(i - kh // 2, 0, rows - nu)
        start0 = jnp.clip(i - kh // 2, 0, rows - kh)
        start1 = jnp.clip(i + 1 - kh // 2, 0, rows - kh)
        lo = jnp.where(second, start1, start0) - ustart
        d0 = ustart - i + (WIN_H - 1)
        qsl = pl.ds(pl.multiple_of(i * GRID_W, 2 * GRID_W), 2 * GRID_W)
        win = pl.ds(pl.multiple_of(ustart * GRID_W, GRID_W), nu * GRID_W)
        q2 = q_ref[qsl, :]
        bias = jnp.concatenate(
            [jnp.where((lo <= r) & (r < lo + kh), b_ref[d0 + r], MASK_VALUE) for r in range(nu)], axis=0)
        t_nb = lax.dot_general(k_ref[win, :], q2, nt, preferred_element_type=_F32) + bias
        t_cx = lax.dot_general(kc, q2, nt, preferred_element_type=_F32)
        return qsl, win, t_nb, t_cx

    def weights(t_nb, t_cx):
        m = jnp.maximum(jnp.max(t_nb, axis=0, keepdims=True), jnp.max(t_cx, axis=0, keepdims=True))
        return (jnp.exp2((t_nb - m) * exp2_scale).astype(_BF16), jnp.exp2((t_cx - m) * exp2_scale).astype(_BF16))

    def step(t, carry):
        staged = [scores(NATTN_PAIRS_PER_STEP * t + g) for g in range(NATTN_PAIRS_PER_STEP)]
        probs = [weights(t_nb, t_cx) for _, _, t_nb, t_cx in staged]
        for (qsl, win, _, _), (e_nb, e_cx) in zip(staged, probs):
            r = lax.dot_general(e_nb, v1_ref[win, :], tn, preferred_element_type=_F32)
            r = r + lax.dot_general(e_cx, vc1, tn, preferred_element_type=_F32)
            o_ref[qsl, :] = (r[:, :HEAD_DIM] / r[:, HEAD_DIM:]).astype(o_ref.dtype)
        return carry

    lax.fori_loop(0, rows // (2 * NATTN_PAIRS_PER_STEP), step, 0)


def _bias_table(rpb):
    j = jnp.arange(GRID_W)
    col_start = jnp.clip(j - WIN_W // 2, 0, GRID_W - WIN_W)
    valid = (j[None, :] >= col_start[:, None]) & (j[None, :] < col_start[:, None] + WIN_W)
    dc = jnp.clip(j[None, :] - j[:, None] + (WIN_W - 1), 0, 2 * WIN_W - 2)
    onehot = (dc[None] == jnp.arange(2 * WIN_W - 1)[:, None, None]).astype(_F32)
    t = jnp.einsum("hdc,cqk->hdqk", rpb, onehot, precision=lax.Precision.HIGHEST)
    t = jnp.where(valid[None, None], t * HEAD_DIM ** 0.5, MASK_VALUE)
    t = jnp.swapaxes(t, 2, 3)
    masked = jnp.full_like(t[:, :1], MASK_VALUE)
    first = jnp.concatenate([t, masked], axis=1)
    second = jnp.concatenate([masked, t], axis=1)
    return jnp.concatenate([first, second], axis=-1).astype(_F32)


def _nattn(qkv, kvc, kc_col, vc_col, bias, batch, seq, n_ctx, n_heads):
    rows = seq // GRID_W
    kh = min(WIN_H, rows)
    assert rows % (2 * NATTN_PAIRS_PER_STEP) == 0
    a = n_heads * HEAD_DIM
    hb = lambda off: off // HEAD_DIM
    lat = lambda off: pl.BlockSpec((None, seq, HEAD_DIM), lambda b, h: (hb(off) + h, b, 0))
    cxt = lambda off: pl.BlockSpec((None, n_ctx, HEAD_DIM), lambda b, h: (hb(off) + h, b, 0))
    return pl.pallas_call(
        functools.partial(_nattn_kernel, rows=rows, kh=kh),
        out_shape=jax.ShapeDtypeStruct((batch * seq, a), _BF16),
        grid=(batch, n_heads),
        in_specs=[
            lat(0), lat(a), lat(2 * a), cxt(kc_col), cxt(vc_col),
            pl.BlockSpec((None, 2 * WIN_H, GRID_W, 2 * GRID_W), lambda b, h: (h, 0, 0, 0)),
        ],
        out_specs=pl.BlockSpec((seq, HEAD_DIM), lambda b, h: (b, h)),
        scratch_shapes=[pltpu.VMEM((seq, 2 * HEAD_DIM), _BF16), pltpu.VMEM((n_ctx, 2 * HEAD_DIM), _BF16)],
        compiler_params=_params("parallel", "parallel"),
        name="nattn",
    )(qkv, qkv, qkv, kvc, kvc, bias)


def _cattn_kernel(q_ref, k_ref, v_ref, o_ref):
    scale = HEAD_DIM ** -0.5
    s = lax.dot_general(q_ref[...], k_ref[...], (((1,), (1,)), ((), ())), preferred_element_type=_F32) * scale
    e = jnp.exp(s - jnp.max(s, axis=-1, keepdims=True))
    p = e * (1.0 / jnp.sum(e, axis=-1, keepdims=True))
    o_ref[...] = _dot(p.astype(_BF16), v_ref[...]).astype(o_ref.dtype)


def _cattn(qkv, batch, n_ctx, n_heads):
    a = n_heads * HEAD_DIM
    spec = lambda off: pl.BlockSpec((None, n_ctx, HEAD_DIM), lambda b, h: (off // HEAD_DIM + h, b, 0))
    return pl.pallas_call(
        _cattn_kernel,
        out_shape=jax.ShapeDtypeStruct((batch * n_ctx, a), _BF16),
        grid=(batch, n_heads),
        in_specs=[spec(0), spec(a), spec(2 * a)],
        out_specs=pl.BlockSpec((n_ctx, HEAD_DIM), lambda b, h: (b, h)),
        compiler_params=_params("parallel", "parallel"),
        name="cattn",
    )(qkv, qkv, qkv)


def _fill_padded(pad_ref, seq, fill_chunk, chunk):
    zeros = jnp.zeros((SEQ_PAD, pad_ref.shape[1]), pad_ref.dtype)
    pad_ref[pl.ds(0, SEQ_PAD), :] = zeros
    pad_ref[pl.ds(SEQ_PAD + seq, SEQ_PAD), :] = zeros

    def body(c, carry):
        t0 = pl.multiple_of(c * chunk, chunk)
        pad_ref[pl.ds(SEQ_PAD + t0, chunk), :] = fill_chunk(t0)
        return carry

    lax.fori_loop(0, seq // chunk, body, 0)


def _pool_kernel(u_ref, w_ref, s_ref, o_ref, pad_ref, *, seq, chunk):
    g = pl.program_id(1)
    _fill_padded(pad_ref, seq, lambda t0: u_ref[pl.ds(t0, chunk), :], chunk)
    w = w_ref[...]
    s = s_ref[...]

    for gi, win in enumerate(POOL_WINDOWS):

        @pl.when(g == gi)
        def _(win=win):
            def body(c, carry):
                t0 = pl.multiple_of(c * chunk, chunk)
                blk = pad_ref[pl.ds(t0, chunk + 2 * SEQ_PAD), :]
                tot = None
                for sft in range(-(win // 2), win - win // 2):
                    part = blk[SEQ_PAD + sft:SEQ_PAD + sft + chunk, :]
                    tot = part if tot is None else tot + part
                t = t0 + lax.broadcasted_iota(jnp.int32, (chunk, blk.shape[1]), 0)
                lo = jnp.maximum(t - win // 2, 0)
                hi = jnp.minimum(t + (win - 1 - win // 2), seq - 1)
                cnt = (hi - lo + 1).astype(_F32)
                pooled = tot / cnt - blk[SEQ_PAD:SEQ_PAD + chunk, :]
                o_ref[pl.ds(t0, chunk), :] = (_dot(pooled.astype(_BF16), w) * s).astype(o_ref.dtype)
                return carry

            lax.fori_loop(0, seq // chunk, body, 0)


def _pool(u, pool_w, pool_scale, batch, seq):
    n_groups, c, _ = pool_w.shape
    chunk = min(128, seq)
    return pl.pallas_call(
        functools.partial(_pool_kernel, seq=seq, chunk=chunk),
        out_shape=jax.ShapeDtypeStruct((batch * seq, n_groups * c), _BF16),
        grid=(batch, n_groups),
        in_specs=[
            pl.BlockSpec((seq, c), lambda b, g: (b, g)),
            pl.BlockSpec((None, c, c), lambda b, g: (g, 0, 0)),
            pl.BlockSpec((1, c), lambda b, g: (0, g)),
        ],
        out_specs=pl.BlockSpec((seq, c), lambda b, g: (b, g)),
        scratch_shapes=[pltpu.VMEM((seq + 2 * SEQ_PAD, c), _F32)],
        compiler_params=_params("parallel", "parallel"),
        name="pool",
    )(u, pool_w, pool_scale.reshape(1, n_groups * c))


def _dwconv_kernel(a_ref, gate_ref, w_ref, b_ref, o_ref, pad_ref, wb_ref, *, seq, chunk):
    def glu(t0):
        return a_ref[pl.ds(t0, chunk), :] * jax.nn.sigmoid(gate_ref[pl.ds(t0, chunk), :])

    _fill_padded(pad_ref, seq, glu, chunk)
    n_ch = o_ref.shape[1]
    for k in range(CONV_KSIZE):
        wb_ref[pl.ds(V7X_SUBLANES * k, V7X_SUBLANES), :] = jnp.broadcast_to(w_ref[k:k + 1, :], (V7X_SUBLANES, n_ch))
    wb_ref[pl.ds(V7X_SUBLANES * CONV_KSIZE, V7X_SUBLANES), :] = jnp.broadcast_to(b_ref[...], (V7X_SUBLANES, n_ch))
    half = CONV_KSIZE // 2
    span = chunk + 2 * SEQ_PAD - V7X_SUBLANES
    groups = chunk // V7X_SUBLANES

    def tap(k):
        return wb_ref[pl.ds(V7X_SUBLANES * k, V7X_SUBLANES), :][None]

    def body(c, carry):
        t0 = pl.multiple_of(c * chunk, chunk)
        blk = pad_ref[pl.ds(t0, chunk + 2 * SEQ_PAD), :]
        acc = jnp.broadcast_to(tap(CONV_KSIZE), (groups, V7X_SUBLANES, n_ch))
        for res in range(V7X_SUBLANES):
            shifted = None
            for k in range(CONV_KSIZE):
                off = SEQ_PAD - half + k
                if off % V7X_SUBLANES != res:
                    continue
                if shifted is None:
                    shifted = blk[res:res + span, :]
                base = off - res
                acc = acc + shifted[base:base + chunk, :].reshape(groups, V7X_SUBLANES, n_ch) * tap(k)
        o_ref[pl.ds(t0, chunk), :] = acc.reshape(chunk, n_ch)
        return carry

    lax.fori_loop(0, seq // chunk, body, 0)


def _dwconv(u, a_col, gate_col, dw_w, dw_b, batch, seq):
    ksz, c = dw_w.shape
    tc = V7X_LANES
    chunk = min(128, seq)
    return pl.pallas_call(
        functools.partial(_dwconv_kernel, seq=seq, chunk=chunk),
        out_shape=jax.ShapeDtypeStruct((batch * seq, c), _F32),
        grid=(batch, c // tc),
        in_specs=[
            pl.BlockSpec((seq, tc), lambda b, j: (b, a_col // tc + j)),
            pl.BlockSpec((seq, tc), lambda b, j: (b, gate_col // tc + j)),
            pl.BlockSpec((ksz, tc), lambda b, j: (0, j)),
            pl.BlockSpec((1, tc), lambda b, j: (0, j)),
        ],
        out_specs=pl.BlockSpec((seq, tc), lambda b, j: (b, j)),
        scratch_shapes=[pltpu.VMEM((seq + 2 * SEQ_PAD, tc), _F32),
                        pltpu.VMEM((V7X_SUBLANES * (ksz + 1), tc), _F32)],
        compiler_params=_params("parallel", "parallel"),
        name="dwconv",
    )(u, u, dw_w, dw_b.reshape(1, c))


def _pw_kernel(h_ref, g_ref, w_ref, o_ref):
    x = h_ref[...]
    y = x * lax.rsqrt(jnp.mean(x * x, axis=-1, keepdims=True) + EPS) * g_ref[...]
    y = y * jax.nn.sigmoid(y)
    o_ref[...] = _dot(y.astype(_BF16), w_ref[...]).astype(o_ref.dtype)


def _pw(h, g, w, tm):
    m, c = h.shape
    return pl.pallas_call(
        _pw_kernel,
        out_shape=jax.ShapeDtypeStruct((m, c), _BF16),
        grid=(m // tm,),
        in_specs=[
            pl.BlockSpec((tm, c), lambda i: (i, 0)),
            pl.BlockSpec((1, c), lambda i: (0, 0)),
            pl.BlockSpec((c, c), lambda i: (0, 0)),
        ],
        out_specs=pl.BlockSpec((tm, c), lambda i: (i, 0)),
        compiler_params=_params("parallel"),
        name="conv_pw",
    )(h, g.reshape(1, c), w)


def _outproj_kernel(yp_ref, yc_ref, ya_ref, w_ref, x_ref, gate_ref, o_ref):
    kp = yp_ref.shape[1]
    kc = yc_ref.shape[1]
    yp, yc, ya = yp_ref[...], yc_ref[...], ya_ref[...]
    for sl in _col_chunks(o_ref.shape[1]):
        acc = _dot(yp, w_ref[pl.ds(0, kp), sl])
        acc = acc + _dot(yc, w_ref[pl.ds(kp, kc), sl])
        acc = acc + _dot(ya, w_ref[pl.ds(kp + kc, ya.shape[1]), sl])
        o_ref[:, sl] = x_ref[:, sl] + gate_ref[:, sl] * acc


def _outproj(yp, yc, ya, w, x, mod, gate_idx, rows, tm, tn, sides=()):
    m, d = x.shape
    tn = _tile(d, tn)
    lhs = lambda y: pl.BlockSpec((tm, y.shape[1]), lambda i, j: (i, 0))
    return _call_with_side_casts(
        _outproj_kernel, (yp, yc, ya, w, x, mod), sides,
        out_shape=jax.ShapeDtypeStruct((m, d), _F32),
        grid=(m // tm, d // tn),
        in_specs=[
            lhs(yp), lhs(yc), lhs(ya),
            pl.BlockSpec((w.shape[0], tn), lambda i, j: (0, j)),
            pl.BlockSpec((tm, tn), lambda i, j: (i, j)),
            _mod_spec(tn, gate_idx, rows, tm),
        ],
        out_specs=pl.BlockSpec((tm, tn), lambda i, j: (i, j)),
        name="outproj",
    )


def _mlp1_kernel(h_ref, w_ref, o_ref):
    h = h_ref[...]
    for sl in _col_chunks(o_ref.shape[1]):
        a = jnp.maximum(_dot(h, w_ref[:, sl]), 0.0)
        o_ref[:, sl] = (a * a).astype(o_ref.dtype)


def _mlp1(h, w, tm, tn, sides=()):
    m, d = h.shape
    f = w.shape[1]
    tn = _tile(f, tn)
    return _call_with_side_casts(
        _mlp1_kernel, (h, w), sides,
        out_shape=jax.ShapeDtypeStruct((m, f), _BF16),
        grid=(m // tm, f // tn),
        in_specs=[
            pl.BlockSpec((tm, d), lambda i, j: (i, 0)),
            pl.BlockSpec((d, tn), lambda i, j: (0, j)),
        ],
        out_specs=pl.BlockSpec((tm, tn), lambda i, j: (i, j)),
        name="mlp1",
    )


def _mlp2_kernel(a_ref, w_ref, x_ref, gate_ref, o_ref, *, nk):
    k = pl.program_id(2)
    chunks = _col_chunks(o_ref.shape[1])

    @pl.when(k == 0)
    def _():
        for sl in chunks:
            o_ref[:, sl] = _dot(a_ref[...], w_ref[:, sl])

    @pl.when(k > 0)
    def _():
        for sl in chunks:
            o_ref[:, sl] += _dot(a_ref[...], w_ref[:, sl])

    @pl.when(k == nk - 1)
    def _():
        o_ref[...] = x_ref[...] + gate_ref[...] * o_ref[...]


def _mlp2(a, w, x, mod, gate_idx, rows, tm, tn, tk):
    m, d = x.shape
    f = a.shape[1]
    tn = _tile(d, tn)
    tk = _tile(f, tk)
    nk = f // tk
    return pl.pallas_call(
        functools.partial(_mlp2_kernel, nk=nk),
        out_shape=jax.ShapeDtypeStruct((m, d), _F32),
        grid=(m // tm, d // tn, nk),
        in_specs=[
            pl.BlockSpec((tm, tk), lambda i, j, k: (i, k)),
            pl.BlockSpec((tk, tn), lambda i, j, k: (k, j)),
            pl.BlockSpec((tm, tn), lambda i, j, k: (i, j)),
            _mod_spec(tn, gate_idx, rows, tm),
        ],
        out_specs=pl.BlockSpec((tm, tn), lambda i, j, k: (i, j)),
        compiler_params=pltpu.CompilerParams(dimension_semantics=("parallel", "parallel", "arbitrary"),
                                             vmem_limit_bytes=V7X_VMEM_LIMIT_BIG_TILES_BYTES),
        name="mlp2",
    )(a, w, x, mod)


def _mixers(u_pc, y_attn, p, batch, seq, tm):
    pool_width = p["pool_w"].shape[0] * p["pool_w"].shape[1]
    conv_width = p["conv_dw_w"].shape[1]
    y_pool = _pool(u_pc, p["pool_w"], p["pool_scale"], batch, seq)
    hconv = _dwconv(u_pc, pool_width, pool_width + conv_width, p["conv_dw_w"], p["conv_dw_b"], batch, seq)
    y_conv = _pw(hconv, p["conv_norm_g"], p["conv_pw_w"], tm)
    return y_pool, y_conv, y_attn


def _post_attention(x, ys, p, mod, rows, tm, raw=None, layer=None, next_w_in=None):
    p = dict(p)
    x, cast = _outproj(*ys, p["w_out"], x, mod, 2, rows, tm, 512,
                       sides=[_SideCast(raw["w_mlp1"], layer)] if raw else ())
    if raw:
        p["w_mlp1"] = cast[0]
    h = _normmod(x, p["norm2_g"], mod, 3, rows)
    sides = []
    if raw:
        sides = [_SideCast(raw["w_mlp2"], layer)] + ([_SideCast(raw["w_in"], layer + 1)] if next_w_in else [])
    a, cast = _mlp1(h, p["w_mlp1"], tm, 1024, sides=sides)
    if raw:
        p["w_mlp2"] = cast[0]
    w_in_next = cast[1] if raw and next_w_in else None
    return _mlp2(a, p["w_mlp2"], x, mod, 5, rows, tm, 1024, 4096), p, w_in_next


def kernel(x, c, ctx, c_ctx, w_ada, b_ada, norm1_g, norm2_g, w_in, pool_w, pool_scale, conv_dw_w, conv_dw_b,
           conv_norm_g, conv_pw_w, q_norm_g, k_norm_g, rpb, w_out, w_mlp1, w_mlp2):
    batch, seq, d = x.shape
    n_ctx = ctx.shape[1]
    depth = w_in.shape[0]
    n_heads = rpb.shape[1]
    attn_w = n_heads * HEAD_DIM
    pool_width = pool_w.shape[1] * pool_w.shape[2]
    conv_width = conv_dw_w.shape[2]
    off_q = pool_width + 2 * conv_width
    off_k = off_q + attn_w
    kh = min(WIN_H, seq // GRID_W)

    m_lat, m_ctx = batch * seq, batch * n_ctx
    tm_lat, tm_ctx = _tile(seq, 1024), _tile(m_ctx, 1024)
    ctx_row = batch
    lat_row = _Rows(base=0, tokens=seq)
    cx_row = _Rows(base=ctx_row, tokens=m_ctx)

    x = x.reshape(m_lat, d)
    xc = ctx.reshape(m_ctx, d)
    n_rows = -(-(batch + 1) // V7X_SUBLANES) * V7X_SUBLANES
    cvec = jnp.zeros((n_rows, d), _F32).at[:batch].set(c).at[ctx_row].set(c_ctx)
    qk_gain = jnp.concatenate([jnp.tile(q_norm_g, (1, n_heads)), jnp.tile(k_norm_g, (1, n_heads))], axis=1)

    raw = {"w_in": w_in, "w_mlp1": w_mlp1, "w_mlp2": w_mlp2}
    w_in_bf = _to_bf16(w_in, 0)
    for layer in range(depth):
        p = {
            "w_in": w_in_bf,
            "pool_w": _to_bf16(pool_w.reshape(depth, -1, pool_w.shape[-1]), layer).reshape(pool_w.shape[1:]),
            "pool_scale": pool_scale[layer],
            "conv_dw_w": conv_dw_w[layer], "conv_dw_b": conv_dw_b[layer],
            "conv_norm_g": conv_norm_g[layer], "conv_pw_w": _to_bf16(conv_pw_w, layer),
            "norm2_g": norm2_g[layer],
        }
        gains = qk_gain[layer:layer + 1]
        update_ctx = layer < depth - 1
        mod = _ada(cvec, w_ada, b_ada, layer).reshape(n_rows, N_MOD, 1, d)
        bias = _bias_table(rpb[layer])

        hc = _normmod(xc, norm1_g[layer], mod, 0, cx_row)
        if update_ctx:
            uc_pc, _ = _proj(hc, p["w_in"], 0, off_q, None, _F32, tm_ctx, 1024, "proj_pc")
            qkv_c, _ = _proj(hc, p["w_in"], off_q, 3 * attn_w, gains, _BF16, tm_ctx, 1024, "proj_qkv")
            kc_col, vc_col = attn_w, 2 * attn_w
        else:
            qkv_c, _ = _proj(hc, p["w_in"], off_k, 2 * attn_w, gains[:, attn_w:], _BF16, tm_ctx, 1024, "proj_kv")
            kc_col, vc_col = 0, attn_w

        h = _normmod(x, norm1_g[layer], mod, 0, lat_row)
        u_pc, _ = _proj(h, p["w_in"], 0, off_q, None, _F32, tm_lat, 1024, "proj_pc")
        qkv, (p["w_out"],) = _proj(h, p["w_in"], off_q, 3 * attn_w, gains, _BF16, tm_lat, 1024, "proj_qkv",
                                   sides=[_SideCast(w_out, layer)])
        y_attn = _nattn(qkv, qkv_c, kc_col, vc_col, bias, batch, seq, n_ctx, n_heads)
        x, p, w_in_bf = _post_attention(x, _mixers(u_pc, y_attn, p, batch, seq, tm_lat), p, mod, lat_row, tm_lat,
                                        raw=raw, layer=layer, next_w_in=layer + 1 < depth)

        if update_ctx:
            yc_attn = _cattn(qkv_c, batch, n_ctx, n_heads)
            xc, _, _ = _post_attention(xc, _mixers(uc_pc, yc_attn, p, batch, n_ctx, tm_ctx), p, mod, cx_row, tm_ctx)

    return x.reshape(batch, seq, d)
```

```python
import functools
import math
from typing import NamedTuple

import jax
import jax.numpy as jnp
from jax import lax
from jax.experimental import pallas as pl
from jax.experimental.pallas import tpu as pltpu

GRID_W = 64
WIN_H = 8
WIN_W = 16
HEAD_DIM = 128
POOL_WINDOWS = (2, 4, 8, 16)
CONV_KSIZE = 31
N_MOD = 6
EPS = 1e-6
MASK_VALUE = -1e30

V7X_LANES = 128
V7X_SUBLANES = 8
V7X_MXU_COLS = 256
V7X_VMEM_LIMIT_BYTES = 56 * 1024 * 1024
V7X_VMEM_LIMIT_BIG_TILES_BYTES = 60 * 1024 * 1024
NATTN_PAIRS_PER_STEP = 4
NATTN_HEADS_PER_STEP = 4
NORMMOD_ROWS = 512
SEQ_PAD = 16

_BF16 = jnp.bfloat16
_F32 = jnp.float32


def _params(*sem):
    return pltpu.CompilerParams(dimension_semantics=sem, vmem_limit_bytes=V7X_VMEM_LIMIT_BYTES)


def _tile(n, pref, *also):
    if n <= pref and not any(a % n for a in also):
        return n
    t = min(pref, n) - min(pref, n) % V7X_LANES
    while n % t or any(a % t for a in also):
        t -= V7X_LANES
    return t


def _dot(a, b):
    return jnp.dot(a, b, preferred_element_type=_F32)


def _cast_kernel(w_ref, o_ref):
    o_ref[...] = w_ref[...].astype(o_ref.dtype)


def _to_bf16(w, layer):
    _, r, c = w.shape
    tr, tc = _tile(r, 512), _tile(c, 2048)
    return pl.pallas_call(
        _cast_kernel,
        out_shape=jax.ShapeDtypeStruct((r, c), _BF16),
        grid=(r // tr, c // tc),
        in_specs=[pl.BlockSpec((None, tr, tc), lambda i, j: (layer, i, j))],
        out_specs=pl.BlockSpec((tr, tc), lambda i, j: (i, j)),
        compiler_params=_params("parallel", "parallel"),
        name="cast_bf16",
    )(w)


class _SideCast(NamedTuple):
    w: jax.Array
    layer: int


def _side_cast_plan(side, grid):
    _, r, c = side.w.shape
    bf16_rows = 2 * V7X_SUBLANES
    nb = max(b for b in range(1, grid[0] * grid[1] + 1) if r % b == 0 and (r // b) % bf16_rows == 0)
    tr = r // nb

    def blk(i, j):
        return jnp.minimum(i * grid[1] + j, nb - 1)

    return (nb,
            pl.BlockSpec((None, tr, c), lambda i, j, *_: (side.layer, blk(i, j), 0)),
            pl.BlockSpec((tr, c), lambda i, j, *_: (blk(i, j), 0)),
            jax.ShapeDtypeStruct((r, c), _BF16))


def _call_with_side_casts(body, args, sides, *, out_shape, grid, in_specs, out_specs, name):
    plans = [_side_cast_plan(s, grid) for s in sides]
    n_in, n_side = len(args), len(sides)

    def wrapped(*refs):
        body(*refs[:n_in], refs[n_in + n_side])
        step = pl.program_id(0) * grid[1] + pl.program_id(1)
        for (nb, _, _, _), src, dst in zip(plans, refs[n_in:n_in + n_side], refs[n_in + n_side + 1:]):
            @pl.when(step < nb)
            def _(src=src, dst=dst):
                dst[...] = src[...].astype(dst.dtype)

    outs = pl.pallas_call(
        wrapped,
        out_shape=[out_shape] + [p[3] for p in plans],
        grid=grid,
        in_specs=list(in_specs) + [p[1] for p in plans],
        out_specs=[out_specs] + [p[2] for p in plans],
        compiler_params=_params(*("arbitrary" if sides else "parallel",) * len(grid)),
        name=name,
    )(*args, *[s.w for s in sides])
    return outs[0], list(outs[1:])


def _ada_kernel(c_ref, w_ref, b_ref, o_ref):
    c = c_ref[...]
    s = (c * jax.nn.sigmoid(c)).astype(_BF16)
    o_ref[...] = _dot(s, w_ref[...].astype(_BF16)) + b_ref[...]


def _ada(cvec, w_ada, b_ada, layer):
    r, d = cvec.shape
    n = w_ada.shape[2]
    tn = _tile(n, 1024)
    return pl.pallas_call(
        _ada_kernel,
        out_shape=jax.ShapeDtypeStruct((r, n), _F32),
        grid=(n // tn,),
        in_specs=[
            pl.BlockSpec((r, d), lambda j: (0, 0)),
            pl.BlockSpec((None, d, tn), lambda j: (layer, 0, j)),
            pl.BlockSpec((None, 1, tn), lambda j: (layer, 0, j)),
        ],
        out_specs=pl.BlockSpec((r, tn), lambda j: (0, j)),
        compiler_params=_params("parallel"),
        name="ada",
    )(cvec, w_ada, b_ada.reshape(b_ada.shape[0], 1, n))


class _Rows(NamedTuple):
    base: int
    tokens: int

    def of_block(self, i, tm):
        return self.base + (i * tm) // self.tokens


def _mod_spec(tn, which, rows, tm):
    return pl.BlockSpec((None, None, 1, tn), lambda i, j=0, *_: (rows.of_block(i, tm), which, 0, j))


def _normmod_kernel(x_ref, g_ref, shift_ref, scale_ref, o_ref):
    group = 2 * V7X_SUBLANES
    gain = g_ref[...]
    one_plus_scale = 1.0 + scale_ref[...]
    shift = shift_ref[...]

    def body(r, carry):
        rs = pl.ds(pl.multiple_of(r * group, group), group)
        x = x_ref[rs, :]
        y = x * lax.rsqrt(jnp.mean(x * x, axis=-1, keepdims=True) + EPS) * gain
        o_ref[rs, :] = (y * one_plus_scale + shift).astype(o_ref.dtype)
        return carry

    lax.fori_loop(0, x_ref.shape[0] // group, body, 0, unroll=4)


def _normmod(x, g, mod, shift_idx, rows):
    m, d = x.shape
    tm = _tile(rows.tokens, NORMMOD_ROWS)
    return pl.pallas_call(
        _normmod_kernel,
        out_shape=jax.ShapeDtypeStruct((m, d), _BF16),
        grid=(m // tm,),
        in_specs=[
            pl.BlockSpec((tm, d), lambda i: (i, 0)),
            pl.BlockSpec((1, d), lambda i: (0, 0)),
            _mod_spec(d, shift_idx, rows, tm),
            _mod_spec(d, shift_idx + 1, rows, tm),
        ],
        out_specs=pl.BlockSpec((tm, d), lambda i: (i, 0)),
        compiler_params=_params("parallel"),
        name="normmod",
    )(x, g.reshape(1, d), mod, mod)


def _col_chunks(n):
    c = V7X_MXU_COLS if n % V7X_MXU_COLS == 0 else n
    return [slice(s, s + c) for s in range(0, n, c)]


def _proj_plain_kernel(h_ref, w_ref, o_ref):
    h = h_ref[...]
    for sl in _col_chunks(o_ref.shape[1]):
        o_ref[:, sl] = _dot(h, w_ref[:, sl]).astype(o_ref.dtype)


def _proj_kernel(h_ref, w_ref, g_ref, o_ref, *, n_norm_blocks):
    normed = pl.program_id(1) < n_norm_blocks
    h = h_ref[...]
    for sl in _col_chunks(w_ref.shape[1]):
        acc = _dot(h, w_ref[:, sl])
        for c in range(0, acc.shape[1], HEAD_DIM):
            a = acc[:, c:c + HEAD_DIM]
            rstd = lax.rsqrt(jnp.mean(a * a, axis=-1, keepdims=True) + EPS)
            gain = g_ref[:, sl.start + c:sl.start + c + HEAD_DIM]
            y = a * jnp.where(normed, rstd, 1.0) * jnp.where(normed, gain, 1.0)
            o_ref[(sl.start + c) // HEAD_DIM] = y.astype(o_ref.dtype)


def _proj(h, w, col0, ncols, gains, out_dtype, tm, tn, name, sides=()):
    m, d = h.shape
    tn = _tile(ncols, tn, col0, *(() if gains is None else (gains.shape[1],)))
    cb = col0 // tn
    in_specs = [
        pl.BlockSpec((tm, d), lambda i, j: (i, 0)),
        pl.BlockSpec((d, tn), lambda i, j: (0, cb + j)),
    ]
    if gains is None:
        body, args = _proj_plain_kernel, (h, w)
        out_shape = jax.ShapeDtypeStruct((m, ncols), out_dtype)
        out_spec = pl.BlockSpec((tm, tn), lambda i, j: (i, j))
    else:
        n_norm_blocks = gains.shape[1] // tn
        body, args = functools.partial(_proj_kernel, n_norm_blocks=n_norm_blocks), (h, w, gains)
        in_specs.append(pl.BlockSpec((1, tn), lambda i, j: (0, jnp.minimum(j, n_norm_blocks - 1))))
        out_shape = jax.ShapeDtypeStruct((ncols // HEAD_DIM, m, HEAD_DIM), out_dtype)
        out_spec = pl.BlockSpec((tn // HEAD_DIM, tm, HEAD_DIM), lambda i, j: (j, i, 0))
    return _call_with_side_casts(
        body, args, sides,
        out_shape=out_shape,
        grid=(m // tm, ncols // tn),
        in_specs=in_specs,
        out_specs=out_spec,
        name=name,
    )


def _nattn_kernel(q_ref, k_ref, v_ref, kc_ref, vc_ref, b_ref, o_ref, v1_ref, vc1_ref, *, rows, kh):
    for g in range(q_ref.shape[0]):
        _nattn_head(q_ref.at[g], k_ref.at[g], v_ref.at[g], kc_ref.at[g], vc_ref.at[g], b_ref.at[g],
                    o_ref.at[:, g * HEAD_DIM:(g + 1) * HEAD_DIM], v1_ref.at[g], vc1_ref.at[g], rows=rows, kh=kh)


def _nattn_head(q_ref, k_ref, v_ref, kc_ref, vc_ref, b_ref, o_ref, v1_ref, vc1_ref, *, rows, kh):
    exp2_scale = HEAD_DIM ** -0.5 * math.log2(math.e)
    nu = min(kh + 1, rows)
    nt = (((1,), (1,)), ((), ()))
    tn = (((0,), (0,)), ((), ()))
    kc = kc_ref[...]
    second = lax.broadcasted_iota(jnp.int32, (1, 2 * GRID_W), 1) >= GRID_W
    for src, dst in ((v_ref, v1_ref), (vc_ref, vc1_ref)):
        dst[:, :HEAD_DIM] = src[...]
        dst[:, HEAD_DIM:] = jnp.ones(src.shape, dst.dtype)
    vc1 = vc1_ref[...]

    def scores(p):
        i = 2 * p
        ustart = jnp.clip(i - kh // 2, 0, rows - nu)
        start0 = jnp.clip(i - kh // 2, 0, rows - kh)
        start1 = jnp.clip(i + 1 - kh // 2, 0, rows - kh)
        lo = jnp.where(second, start1, start0) - ustart
        d0 = ustart - i + (WIN_H - 1)
        qsl = pl.ds(pl.multiple_of(i * GRID_W, 2 * GRID_W), 2 * GRID_W)
        win = pl.ds(pl.multiple_of(ustart * GRID_W, GRID_W), nu * GRID_W)
        q2 = q_ref[qsl, :]
        bias = jnp.concatenate(
            [jnp.where((lo <= r) & (r < lo + kh), b_ref[d0 + r], MASK_VALUE) for r in range(nu)], axis=0)
        t_nb = lax.dot_general(k_ref[win, :], q2, nt, preferred_element_type=_F32) + bias
        t_cx = lax.dot_general(kc, q2, nt, preferred_element_type=_F32)
        return qsl, win, t_nb, t_cx

    def weights(t_nb, t_cx):
        m = jnp.maximum(jnp.max(t_nb, axis=0, keepdims=True), jnp.max(t_cx, axis=0, keepdims=True))
        return (jnp.exp2((t_nb - m) * exp2_scale).astype(_BF16), jnp.exp2((t_cx - m) * exp2_scale).astype(_BF16))

    def step(t, carry):
        staged = [scores(NATTN_PAIRS_PER_STEP * t + g) for g in range(NATTN_PAIRS_PER_STEP)]
        probs = [weights(t_nb, t_cx) for _, _, t_nb, t_cx in staged]
        for (qsl, win, _, _), (e_nb, e_cx) in zip(staged, probs):
            r = lax.dot_general(e_nb, v1_ref[win, :], tn, preferred_element_type=_F32)
            r = r + lax.dot_general(e_cx, vc1, tn, preferred_element_type=_F32)
            o_ref[qsl, :] = (r[:, :HEAD_DIM] / r[:, HEAD_DIM:]).astype(o_ref.dtype)
        return carry

    lax.fori_loop(0, rows // (2 * NATTN_PAIRS_PER_STEP), step, 0)


def _bias_table(rpb):
    j = jnp.arange(GRID_W)
    col_start = jnp.clip(j - WIN_W // 2, 0, GRID_W - WIN_W)
    valid = (j[None, :] >= col_start[:, None]) & (j[None, :] < col_start[:, None] + WIN_W)
    dc = jnp.clip(j[None, :] - j[:, None] + (WIN_W - 1), 0, 2 * WIN_W - 2)
    onehot = (dc[None] == jnp.arange(2 * WIN_W - 1)[:, None, None]).astype(_F32)
    t = jnp.einsum("hdc,cqk->hdqk", rpb, onehot, precision=lax.Precision.HIGHEST)
    t = jnp.where(valid[None, None], t * HEAD_DIM ** 0.5, MASK_VALUE)
    t = jnp.swapaxes(t, 2, 3)
    masked = jnp.full_like(t[:, :1], MASK_VALUE)
    first = jnp.concatenate([t, masked], axis=1)
    second = jnp.concatenate([masked, t], axis=1)
    return jnp.concatenate([first, second], axis=-1).astype(_F32)


def _nattn(qkv, kvc, kc_col, vc_col, bias, batch, seq, n_ctx, n_heads):
    rows = seq // GRID_W
    kh = min(WIN_H, rows)
    g = NATTN_HEADS_PER_STEP
    assert rows % (2 * NATTN_PAIRS_PER_STEP) == 0 and n_heads % g == 0
    a = n_heads * HEAD_DIM
    hb = lambda off: off // (g * HEAD_DIM)
    lat = lambda off: pl.BlockSpec((g, seq, HEAD_DIM), lambda b, h: (hb(off) + h, b, 0))
    cxt = lambda off: pl.BlockSpec((g, n_ctx, HEAD_DIM), lambda b, h: (hb(off) + h, b, 0))
    return pl.pallas_call(
        functools.partial(_nattn_kernel, rows=rows, kh=kh),
        out_shape=jax.ShapeDtypeStruct((batch * seq, a), _BF16),
        grid=(batch, n_heads // g),
        in_specs=[
            lat(0), lat(a), lat(2 * a), cxt(kc_col), cxt(vc_col),
            pl.BlockSpec((g, 2 * WIN_H, GRID_W, 2 * GRID_W), lambda b, h: (h, 0, 0, 0)),
        ],
        out_specs=pl.BlockSpec((seq, g * HEAD_DIM), lambda b, h: (b, h)),
        scratch_shapes=[pltpu.VMEM((g, seq, 2 * HEAD_DIM), _BF16), pltpu.VMEM((g, n_ctx, 2 * HEAD_DIM), _BF16)],
        compiler_params=_params("parallel", "parallel"),
        name="nattn",
    )(qkv, qkv, qkv, kvc, kvc, bias)


def _cattn_kernel(q_ref, k_ref, v_ref, o_ref):
    scale = HEAD_DIM ** -0.5
    s = lax.dot_general(q_ref[...], k_ref[...], (((1,), (1,)), ((), ())), preferred_element_type=_F32) * scale
    e = jnp.exp(s - jnp.max(s, axis=-1, keepdims=True))
    p = e * (1.0 / jnp.sum(e, axis=-1, keepdims=True))
    o_ref[...] = _dot(p.astype(_BF16), v_ref[...]).astype(o_ref.dtype)


def _cattn(qkv, batch, n_ctx, n_heads):
    a = n_heads * HEAD_DIM
    spec = lambda off: pl.BlockSpec((None, n_ctx, HEAD_DIM), lambda b, h: (off // HEAD_DIM + h, b, 0))
    return pl.pallas_call(
        _cattn_kernel,
        out_shape=jax.ShapeDtypeStruct((batch * n_ctx, a), _BF16),
        grid=(batch, n_heads),
        in_specs=[spec(0), spec(a), spec(2 * a)],
        out_specs=pl.BlockSpec((n_ctx, HEAD_DIM), lambda b, h: (b, h)),
        compiler_params=_params("parallel", "parallel"),
        name="cattn",
    )(qkv, qkv, qkv)


def _fill_padded(pad_ref, seq, fill_chunk, chunk):
    zeros = jnp.zeros((SEQ_PAD, pad_ref.shape[1]), pad_ref.dtype)
    pad_ref[pl.ds(0, SEQ_PAD), :] = zeros
    pad_ref[pl.ds(SEQ_PAD + seq, SEQ_PAD), :] = zeros

    def body(c, carry):
        t0 = pl.multiple_of(c * chunk, chunk)
        pad_ref[pl.ds(SEQ_PAD + t0, chunk), :] = fill_chunk(t0)
        return carry

    lax.fori_loop(0, seq // chunk, body, 0)


def _pool_kernel(u_ref, w_ref, s_ref, o_ref, pad_ref, *, seq, chunk):
    g = pl.program_id(1)
    _fill_padded(pad_ref, seq, lambda t0: u_ref[pl.ds(t0, chunk), :], chunk)
    w = w_ref[...]
    s = s_ref[...]

    for gi, win in enumerate(POOL_WINDOWS):

        @pl.when(g == gi)
        def _(win=win):
            def body(c, carry):
                t0 = pl.multiple_of(c * chunk, chunk)
                blk = pad_ref[pl.ds(t0, chunk + 2 * SEQ_PAD), :]
                run, step = blk, 1
                while step < win:
                    run = run + pltpu.roll(run, run.shape[0] - step, axis=0)
                    step *= 2
                tot = run[SEQ_PAD - win // 2:SEQ_PAD - win // 2 + chunk, :]
                t = t0 + lax.broadcasted_iota(jnp.int32, (chunk, blk.shape[1]), 0)
                lo = jnp.maximum(t - win // 2, 0)
                hi = jnp.minimum(t + (win - 1 - win // 2), seq - 1)
                cnt = (hi - lo + 1).astype(_F32)
                pooled = tot / cnt - blk[SEQ_PAD:SEQ_PAD + chunk, :]
                o_ref[pl.ds(t0, chunk), :] = (_dot(pooled.astype(_BF16), w) * s).astype(o_ref.dtype)
                return carry

            lax.fori_loop(0, seq // chunk, body, 0)


def _pool(u, pool_w, pool_scale, batch, seq):
    n_groups, c, _ = pool_w.shape
    chunk = min(128, seq)
    return pl.pallas_call(
        functools.partial(_pool_kernel, seq=seq, chunk=chunk),
        out_shape=jax.ShapeDtypeStruct((batch * seq, n_groups * c), _BF16),
        grid=(batch, n_groups),
        in_specs=[
            pl.BlockSpec((seq, c), lambda b, g: (b, g)),
            pl.BlockSpec((None, c, c), lambda b, g: (g, 0, 0)),
            pl.BlockSpec((1, c), lambda b, g: (0, g)),
        ],
        out_specs=pl.BlockSpec((seq, c), lambda b, g: (b, g)),
        scratch_shapes=[pltpu.VMEM((seq + 2 * SEQ_PAD, c), _F32)],
        compiler_params=_params("parallel", "parallel"),
        name="pool",
    )(u, pool_w, pool_scale.reshape(1, n_groups * c))


def _dwconv_kernel(a_ref, gate_ref, w_ref, b_ref, o_ref, pad_ref, wb_ref, *, seq, chunk):
    def glu(t0):
        return a_ref[pl.ds(t0, chunk), :] * jax.nn.sigmoid(gate_ref[pl.ds(t0, chunk), :])

    _fill_padded(pad_ref, seq, glu, chunk)
    n_ch = o_ref.shape[1]
    for k in range(CONV_KSIZE):
        wb_ref[pl.ds(V7X_SUBLANES * k, V7X_SUBLANES), :] = jnp.broadcast_to(w_ref[k:k + 1, :], (V7X_SUBLANES, n_ch))
    wb_ref[pl.ds(V7X_SUBLANES * CONV_KSIZE, V7X_SUBLANES), :] = jnp.broadcast_to(b_ref[...], (V7X_SUBLANES, n_ch))
    half = CONV_KSIZE // 2
    span = chunk + 2 * SEQ_PAD - V7X_SUBLANES
    groups = chunk // V7X_SUBLANES

    def tap(k):
        return wb_ref[pl.ds(V7X_SUBLANES * k, V7X_SUBLANES), :][None]

    def body(c, carry):
        t0 = pl.multiple_of(c * chunk, chunk)
        blk = pad_ref[pl.ds(t0, chunk + 2 * SEQ_PAD), :]
        acc = jnp.broadcast_to(tap(CONV_KSIZE), (groups, V7X_SUBLANES, n_ch))
        for res in range(V7X_SUBLANES):
            shifted = None
            for k in range(CONV_KSIZE):
                off = SEQ_PAD - half + k
                if off % V7X_SUBLANES != res:
                    continue
                if shifted is None:
                    shifted = blk[res:res + span, :]
                base = off - res
                acc = acc + shifted[base:base + chunk, :].reshape(groups, V7X_SUBLANES, n_ch) * tap(k)
        o_ref[pl.ds(t0, chunk), :] = acc.reshape(chunk, n_ch)
        return carry

    lax.fori_loop(0, seq // chunk, body, 0)


def _dwconv(u, a_col, gate_col, dw_w, dw_b, batch, seq):
    ksz, c = dw_w.shape
    tc = V7X_LANES
    chunk = min(128, seq)
    return pl.pallas_call(
        functools.partial(_dwconv_kernel, seq=seq, chunk=chunk),
        out_shape=jax.ShapeDtypeStruct((batch * seq, c), _F32),
        grid=(batch, c // tc),
        in_specs=[
            pl.BlockSpec((seq, tc), lambda b, j: (b, a_col // tc + j)),
            pl.BlockSpec((seq, tc), lambda b, j: (b, gate_col // tc + j)),
            pl.BlockSpec((ksz, tc), lambda b, j: (0, j)),
            pl.BlockSpec((1, tc), lambda b, j: (0, j)),
        ],
        out_specs=pl.BlockSpec((seq, tc), lambda b, j: (b, j)),
        scratch_shapes=[pltpu.VMEM((seq + 2 * SEQ_PAD, tc), _F32),
                        pltpu.VMEM((V7X_SUBLANES * (ksz + 1), tc), _F32)],
        compiler_params=_params("parallel", "parallel"),
        name="dwconv",
    )(u, u, dw_w, dw_b.reshape(1, c))


def _pw_kernel(h_ref, g_ref, w_ref, o_ref):
    x = h_ref[...]
    y = x * lax.rsqrt(jnp.mean(x * x, axis=-1, keepdims=True) + EPS) * g_ref[...]
    y = y * jax.nn.sigmoid(y)
    o_ref[...] = _dot(y.astype(_BF16), w_ref[...]).astype(o_ref.dtype)


def _pw(h, g, w, tm):
    m, c = h.shape
    return pl.pallas_call(
        _pw_kernel,
        out_shape=jax.ShapeDtypeStruct((m, c), _BF16),
        grid=(m // tm,),
        in_specs=[
            pl.BlockSpec((tm, c), lambda i: (i, 0)),
            pl.BlockSpec((1, c), lambda i: (0, 0)),
            pl.BlockSpec((c, c), lambda i: (0, 0)),
        ],
        out_specs=pl.BlockSpec((tm, c), lambda i: (i, 0)),
        compiler_params=_params("parallel"),
        name="conv_pw",
    )(h, g.reshape(1, c), w)


def _outproj_kernel(yp_ref, yc_ref, ya_ref, w_ref, x_ref, gate_ref, o_ref):
    kp = yp_ref.shape[1]
    kc = yc_ref.shape[1]
    yp, yc, ya = yp_ref[...], yc_ref[...], ya_ref[...]
    for sl in _col_chunks(o_ref.shape[1]):
        acc = _dot(yp, w_ref[pl.ds(0, kp), sl])
        acc = acc + _dot(yc, w_ref[pl.ds(kp, kc), sl])
        acc = acc + _dot(ya, w_ref[pl.ds(kp + kc, ya.shape[1]), sl])
        o_ref[:, sl] = x_ref[:, sl] + gate_ref[:, sl] * acc


def _outproj(yp, yc, ya, w, x, mod, gate_idx, rows, tm, tn, sides=()):
    m, d = x.shape
    tn = _tile(d, tn)
    lhs = lambda y: pl.BlockSpec((tm, y.shape[1]), lambda i, j: (i, 0))
    return _call_with_side_casts(
        _outproj_kernel, (yp, yc, ya, w, x, mod), sides,
        out_shape=jax.ShapeDtypeStruct((m, d), _F32),
        grid=(m // tm, d // tn),
        in_specs=[
            lhs(yp), lhs(yc), lhs(ya),
            pl.BlockSpec((w.shape[0], tn), lambda i, j: (0, j)),
            pl.BlockSpec((tm, tn), lambda i, j: (i, j)),
            _mod_spec(tn, gate_idx, rows, tm),
        ],
        out_specs=pl.BlockSpec((tm, tn), lambda i, j: (i, j)),
        name="outproj",
    )


def _mlp1_kernel(h_ref, w_ref, o_ref):
    h = h_ref[...]
    for sl in _col_chunks(o_ref.shape[1]):
        a = jnp.maximum(_dot(h, w_ref[:, sl]), 0.0)
        o_ref[:, sl] = (a * a).astype(o_ref.dtype)


def _mlp1(h, w, tm, tn, sides=()):
    m, d = h.shape
    f = w.shape[1]
    tn = _tile(f, tn)
    return _call_with_side_casts(
        _mlp1_kernel, (h, w), sides,
        out_shape=jax.ShapeDtypeStruct((m, f), _BF16),
        grid=(m // tm, f // tn),
        in_specs=[
            pl.BlockSpec((tm, d), lambda i, j: (i, 0)),
            pl.BlockSpec((d, tn), lambda i, j: (0, j)),
        ],
        out_specs=pl.BlockSpec((tm, tn), lambda i, j: (i, j)),
        name="mlp1",
    )


def _mlp2_kernel(a_ref, w_ref, x_ref, gate_ref, o_ref, *, nk):
    k = pl.program_id(2)
    chunks = _col_chunks(o_ref.shape[1])

    @pl.when(k == 0)
    def _():
        for sl in chunks:
            o_ref[:, sl] = _dot(a_ref[...], w_ref[:, sl])

    @pl.when(k > 0)
    def _():
        for sl in chunks:
            o_ref[:, sl] += _dot(a_ref[...], w_ref[:, sl])

    @pl.when(k == nk - 1)
    def _():
        o_ref[...] = x_ref[...] + gate_ref[...] * o_ref[...]


def _mlp2(a, w, x, mod, gate_idx, rows, tm, tn, tk):
    m, d = x.shape
    f = a.shape[1]
    tn = _tile(d, tn)
    tk = _tile(f, tk)
    nk = f // tk
    return pl.pallas_call(
        functools.partial(_mlp2_kernel, nk=nk),
        out_shape=jax.ShapeDtypeStruct((m, d), _F32),
        grid=(m // tm, d // tn, nk),
        in_specs=[
            pl.BlockSpec((tm, tk), lambda i, j, k: (i, k)),
            pl.BlockSpec((tk, tn), lambda i, j, k: (k, j)),
            pl.BlockSpec((tm, tn), lambda i, j, k: (i, j)),
            _mod_spec(tn, gate_idx, rows, tm),
        ],
        out_specs=pl.BlockSpec((tm, tn), lambda i, j, k: (i, j)),
        compiler_params=pltpu.CompilerParams(dimension_semantics=("parallel", "parallel", "arbitrary"),
                                             vmem_limit_bytes=V7X_VMEM_LIMIT_BIG_TILES_BYTES),
        name="mlp2",
    )(a, w, x, mod)


def _mixers(u_pc, y_attn, p, batch, seq, tm):
    pool_width = p["pool_w"].shape[0] * p["pool_w"].shape[1]
    conv_width = p["conv_dw_w"].shape[1]
    y_pool = _pool(u_pc, p["pool_w"], p["pool_scale"], batch, seq)
    hconv = _dwconv(u_pc, pool_width, pool_width + conv_width, p["conv_dw_w"], p["conv_dw_b"], batch, seq)
    y_conv = _pw(hconv, p["conv_norm_g"], p["conv_pw_w"], tm)
    return y_pool, y_conv, y_attn


def _post_attention(x, ys, p, mod, rows, tm, raw=None, layer=None, next_w_in=None):
    p = dict(p)
    x, cast = _outproj(*ys, p["w_out"], x, mod, 2, rows, tm, 512,
                       sides=[_SideCast(raw["w_mlp1"], layer)] if raw else ())
    if raw:
        p["w_mlp1"] = cast[0]
    h = _normmod(x, p["norm2_g"], mod, 3, rows)
    sides = []
    if raw:
        sides = [_SideCast(raw["w_mlp2"], layer)] + ([_SideCast(raw["w_in"], layer + 1)] if next_w_in else [])
    a, cast = _mlp1(h, p["w_mlp1"], tm, 1024, sides=sides)
    if raw:
        p["w_mlp2"] = cast[0]
    w_in_next = cast[1] if raw and next_w_in else None
    return _mlp2(a, p["w_mlp2"], x, mod, 5, rows, tm, 1024, 4096), p, w_in_next


def kernel(x, c, ctx, c_ctx, w_ada, b_ada, norm1_g, norm2_g, w_in, pool_w, pool_scale, conv_dw_w, conv_dw_b,
           conv_norm_g, conv_pw_w, q_norm_g, k_norm_g, rpb, w_out, w_mlp1, w_mlp2):
    batch, seq, d = x.shape
    n_ctx = ctx.shape[1]
    depth = w_in.shape[0]
    n_heads = rpb.shape[1]
    attn_w = n_heads * HEAD_DIM
    pool_width = pool_w.shape[1] * pool_w.shape[2]
    conv_width = conv_dw_w.shape[2]
    off_q = pool_width + 2 * conv_width
    off_k = off_q + attn_w
    kh = min(WIN_H, seq // GRID_W)

    m_lat, m_ctx = batch * seq, batch * n_ctx
    tm_lat, tm_ctx = _tile(seq, 1024), _tile(m_ctx, 1024)
    ctx_row = batch
    lat_row = _Rows(base=0, tokens=seq)
    cx_row = _Rows(base=ctx_row, tokens=m_ctx)

    x = x.reshape(m_lat, d)
    xc = ctx.reshape(m_ctx, d)
    n_rows = -(-(batch + 1) // V7X_SUBLANES) * V7X_SUBLANES
    cvec = jnp.zeros((n_rows, d), _F32).at[:batch].set(c).at[ctx_row].set(c_ctx)
    qk_gain = jnp.concatenate([jnp.tile(q_norm_g, (1, n_heads)), jnp.tile(k_norm_g, (1, n_heads))], axis=1)

    raw = {"w_in": w_in, "w_mlp1": w_mlp1, "w_mlp2": w_mlp2}
    w_in_bf = _to_bf16(w_in, 0)
    for layer in range(depth):
        p = {
            "w_in": w_in_bf,
            "pool_w": _to_bf16(pool_w.reshape(depth, -1, pool_w.shape[-1]), layer).reshape(pool_w.shape[1:]),
            "pool_scale": pool_scale[layer],
            "conv_dw_w": conv_dw_w[layer], "conv_dw_b": conv_dw_b[layer],
            "conv_norm_g": conv_norm_g[layer], "conv_pw_w": _to_bf16(conv_pw_w, layer),
            "norm2_g": norm2_g[layer],
        }
        gains = qk_gain[layer:layer + 1]
        update_ctx = layer < depth - 1
        mod = _ada(cvec, w_ada, b_ada, layer).reshape(n_rows, N_MOD, 1, d)
        bias = _bias_table(rpb[layer])

        hc = _normmod(xc, norm1_g[layer], mod, 0, cx_row)
        if update_ctx:
            uc_pc, _ = _proj(hc, p["w_in"], 0, off_q, None, _F32, tm_ctx, 1024, "proj_pc")
            qkv_c, _ = _proj(hc, p["w_in"], off_q, 3 * attn_w, gains, _BF16, tm_ctx, 1024, "proj_qkv")
            kc_col, vc_col = attn_w, 2 * attn_w
        else:
            qkv_c, _ = _proj(hc, p["w_in"], off_k, 2 * attn_w, gains[:, attn_w:], _BF16, tm_ctx, 1024, "proj_kv")
            kc_col, vc_col = 0, attn_w

        h = _normmod(x, norm1_g[layer], mod, 0, lat_row)
        u_pc, _ = _proj(h, p["w_in"], 0, off_q, None, _F32, tm_lat, 1024, "proj_pc")
        qkv, (p["w_out"],) = _proj(h, p["w_in"], off_q, 3 * attn_w, gains, _BF16, tm_lat, 1024, "proj_qkv",
                                   sides=[_SideCast(w_out, layer)])
        y_attn = _nattn(qkv, qkv_c, kc_col, vc_col, bias, batch, seq, n_ctx, n_heads)
        x, p, w_in_bf = _post_attention(x, _mixers(u_pc, y_attn, p, batch, seq, tm_lat), p, mod, lat_row, tm_lat,
                                        raw=raw, layer=layer, next_w_in=layer + 1 < depth)

        if update_ctx:
            yc_attn = _cattn(qkv_c, batch, n_ctx, n_heads)
            xc, _, _ = _post_attention(xc, _mixers(uc_pc, yc_attn, p, batch, n_ctx, tm_ctx), p, mod, cx_row, tm_ctx)

    return x.reshape(batch, seq, d)
```

```python
import functools
import math
from typing import NamedTuple

import jax
import jax.numpy as jnp
from jax import lax
from jax.experimental import pallas as pl
from jax.experimental.pallas import tpu as pltpu

GRID_W = 64
WIN_H = 8
WIN_W = 16
HEAD_DIM = 128
POOL_WINDOWS = (2, 4, 8, 16)
CONV_KSIZE = 31
N_MOD = 6
EPS = 1e-6
MASK_VALUE = -1e30

V7X_LANES = 128
V7X_SUBLANES = 8
V7X_MXU_COLS = 256
V7X_VMEM_LIMIT_BYTES = 56 * 1024 * 1024
V7X_VMEM_LIMIT_BIG_TILES_BYTES = 60 * 1024 * 1024
MATMUL_ROWS = 1024
PROJ_COLS = 1024
OUTPROJ_COLS = 512
MLP1_COLS = 1024
MLP2_COLS = 1024
MLP2_DEPTH = 4096
NATTN_PAIRS_PER_STEP = 2
NATTN_HEADS_PER_STEP = 4
NORMMOD_ROWS = 512
SEQ_PAD = 16

_BF16 = jnp.bfloat16
_F32 = jnp.float32


def _params(*sem):
    return pltpu.CompilerParams(dimension_semantics=sem, vmem_limit_bytes=V7X_VMEM_LIMIT_BYTES)


def _tile(n, pref, *also):
    if n <= pref and not any(a % n for a in also):
        return n
    t = min(pref, n) - min(pref, n) % V7X_LANES
    while n % t or any(a % t for a in also):
        t -= V7X_LANES
    return t


def _dot(a, b):
    return jnp.dot(a, b, preferred_element_type=_F32)


def _cast_kernel(w_ref, o_ref):
    o_ref[...] = w_ref[...].astype(o_ref.dtype)


def _to_bf16(w, layer):
    _, r, c = w.shape
    tr, tc = _tile(r, 512), _tile(c, 2048)
    return pl.pallas_call(
        _cast_kernel,
        out_shape=jax.ShapeDtypeStruct((r, c), _BF16),
        grid=(r // tr, c // tc),
        in_specs=[pl.BlockSpec((None, tr, tc), lambda i, j: (layer, i, j))],
        out_specs=pl.BlockSpec((tr, tc), lambda i, j: (i, j)),
        compiler_params=_params("parallel", "parallel"),
        name="cast_bf16",
    )(w)


class _SideCast(NamedTuple):
    w: jax.Array
    layer: int


def _linear_step(idx, grid):
    step = idx[0]
    for i, n in zip(idx[1:], grid[1:]):
        step = step * n + i
    return step


def _side_cast_plan(side, grid):
    _, r, c = side.w.shape
    bf16_rows = 2 * V7X_SUBLANES
    nb = max(b for b in range(1, math.prod(grid) + 1) if r % b == 0 and (r // b) % bf16_rows == 0)
    tr = r // nb

    def blk(*idx):
        return jnp.minimum(_linear_step(idx, grid), nb - 1)

    return (nb,
            pl.BlockSpec((None, tr, c), lambda *idx: (side.layer, blk(*idx), 0)),
            pl.BlockSpec((tr, c), lambda *idx: (blk(*idx), 0)),
            jax.ShapeDtypeStruct((r, c), _BF16))


def _call_with_side_casts(body, args, sides, *, out_shape, grid, in_specs, out_specs, name, semantics=None,
                          vmem_limit_bytes=V7X_VMEM_LIMIT_BYTES):
    plans = [_side_cast_plan(s, grid) for s in sides]
    n_in, n_side = len(args), len(sides)

    def wrapped(*refs):
        def side():
            step = _linear_step([pl.program_id(a) for a in range(len(grid))], grid)
            for (nb, _, _, _), src, dst in zip(plans, refs[n_in:n_in + n_side], refs[n_in + n_side + 1:]):
                if nb == math.prod(grid):
                    dst[...] = src[...].astype(dst.dtype)
                else:
                    @pl.when(step < nb)
                    def _(src=src, dst=dst):
                        dst[...] = src[...].astype(dst.dtype)

        body(*refs[:n_in], refs[n_in + n_side], side=side)

    if semantics is None or sides:
        semantics = ("arbitrary" if sides else "parallel",) * len(grid)
    outs = pl.pallas_call(
        wrapped,
        out_shape=[out_shape] + [p[3] for p in plans],
        grid=grid,
        in_specs=list(in_specs) + [p[1] for p in plans],
        out_specs=[out_specs] + [p[2] for p in plans],
        compiler_params=pltpu.CompilerParams(dimension_semantics=semantics, vmem_limit_bytes=vmem_limit_bytes),
        name=name,
    )(*args, *[s.w for s in sides])
    return outs[0], list(outs[1:])


def _ada_kernel(c_ref, w_ref, b_ref, o_ref):
    c = c_ref[...]
    s = (c * jax.nn.sigmoid(c)).astype(_BF16)
    o_ref[...] = _dot(s, w_ref[...].astype(_BF16)) + b_ref[...]


def _ada(cvec, w_ada, b_ada, layer):
    r, d = cvec.shape
    n = w_ada.shape[2]
    tn = _tile(n, 1024)
    return pl.pallas_call(
        _ada_kernel,
        out_shape=jax.ShapeDtypeStruct((r, n), _F32),
        grid=(n // tn,),
        in_specs=[
            pl.BlockSpec((r, d), lambda j: (0, 0)),
            pl.BlockSpec((None, d, tn), lambda j: (layer, 0, j)),
            pl.BlockSpec((None, 1, tn), lambda j: (layer, 0, j)),
        ],
        out_specs=pl.BlockSpec((r, tn), lambda j: (0, j)),
        compiler_params=_params("parallel"),
        name="ada",
    )(cvec, w_ada, b_ada.reshape(b_ada.shape[0], 1, n))


class _Rows(NamedTuple):
    base: int
    tokens: int

    def of_block(self, i, tm):
        return self.base + (i * tm) // self.tokens


def _mod_spec(tn, which, rows, tm):
    return pl.BlockSpec((None, None, 1, tn), lambda i, j=0, *_: (rows.of_block(i, tm), which, 0, j))


def _normmod_kernel(x_ref, g_ref, shift_ref, scale_ref, o_ref):
    group = 2 * V7X_SUBLANES
    gain = g_ref[...]
    one_plus_scale = 1.0 + scale_ref[...]
    shift = shift_ref[...]

    def body(r, carry):
        rs = pl.ds(pl.multiple_of(r * group, group), group)
        x = x_ref[rs, :]
        y = x * lax.rsqrt(jnp.mean(x * x, axis=-1, keepdims=True) + EPS) * gain
        o_ref[rs, :] = (y * one_plus_scale + shift).astype(o_ref.dtype)
        return carry

    lax.fori_loop(0, x_ref.shape[0] // group, body, 0, unroll=4)


def _normmod(x, g, mod, shift_idx, rows):
    m, d = x.shape
    tm = _tile(rows.tokens, NORMMOD_ROWS)
    return pl.pallas_call(
        _normmod_kernel,
        out_shape=jax.ShapeDtypeStruct((m, d), _BF16),
        grid=(m // tm,),
        in_specs=[
            pl.BlockSpec((tm, d), lambda i: (i, 0)),
            pl.BlockSpec((1, d), lambda i: (0, 0)),
            _mod_spec(d, shift_idx, rows, tm),
            _mod_spec(d, shift_idx + 1, rows, tm),
        ],
        out_specs=pl.BlockSpec((tm, d), lambda i: (i, 0)),
        compiler_params=_params("parallel"),
        name="normmod",
    )(x, g.reshape(1, d), mod, mod)


def _col_chunks(n):
    c = V7X_MXU_COLS if n % V7X_MXU_COLS == 0 else n
    return [slice(s, s + c) for s in range(0, n, c)]


def _proj_plain_kernel(h_ref, w_ref, o_ref, *, side):
    h = h_ref[...]
    for sl in _col_chunks(o_ref.shape[1]):
        o_ref[:, sl] = _dot(h, w_ref[:, sl]).astype(o_ref.dtype)
    side()


def _proj_kernel(h_ref, w_ref, g_ref, o_ref, *, n_norm_blocks, side):
    normed = pl.program_id(1) < n_norm_blocks
    h = h_ref[...]
    for sl in _col_chunks(w_ref.shape[1]):
        acc = _dot(h, w_ref[:, sl])
        for c in range(0, acc.shape[1], HEAD_DIM):
            a = acc[:, c:c + HEAD_DIM]
            rstd = lax.rsqrt(jnp.mean(a * a, axis=-1, keepdims=True) + EPS)
            gain = g_ref[:, sl.start + c:sl.start + c + HEAD_DIM]
            y = a * jnp.where(normed, rstd, 1.0) * jnp.where(normed, gain, 1.0)
            o_ref[(sl.start + c) // HEAD_DIM] = y.astype(o_ref.dtype)
    side()


def _proj(h, w, col0, ncols, gains, out_dtype, tm, tn, name, sides=()):
    m, d = h.shape
    tn = _tile(ncols, tn, col0, *(() if gains is None else (gains.shape[1],)))
    cb = col0 // tn
    in_specs = [
        pl.BlockSpec((tm, d), lambda i, j: (i, 0)),
        pl.BlockSpec((d, tn), lambda i, j: (0, cb + j)),
    ]
    if gains is None:
        body, args = _proj_plain_kernel, (h, w)
        out_shape = jax.ShapeDtypeStruct((m, ncols), out_dtype)
        out_spec = pl.BlockSpec((tm, tn), lambda i, j: (i, j))
    else:
        n_norm_blocks = gains.shape[1] // tn
        body, args = functools.partial(_proj_kernel, n_norm_blocks=n_norm_blocks), (h, w, gains)
        in_specs.append(pl.BlockSpec((1, tn), lambda i, j: (0, jnp.minimum(j, n_norm_blocks - 1))))
        out_shape = jax.ShapeDtypeStruct((ncols // HEAD_DIM, m, HEAD_DIM), out_dtype)
        out_spec = pl.BlockSpec((tn // HEAD_DIM, tm, HEAD_DIM), lambda i, j: (j, i, 0))
    return _call_with_side_casts(
        body, args, sides,
        out_shape=out_shape,
        grid=(m // tm, ncols // tn),
        in_specs=in_specs,
        out_specs=out_spec,
        name=name,
    )


def _nattn_kernel(q_ref, k_ref, v_ref, kc_ref, vc_ref, b_ref, o_ref, v1_ref, vc1_ref, *, rows, kh):
    for g in range(q_ref.shape[0]):
        _nattn_head(q_ref.at[g], k_ref.at[g], v_ref.at[g], kc_ref.at[g], vc_ref.at[g], b_ref.at[g],
                    o_ref.at[:, g * HEAD_DIM:(g + 1) * HEAD_DIM], v1_ref.at[g], vc1_ref.at[g], rows=rows, kh=kh)


def _nattn_head(q_ref, k_ref, v_ref, kc_ref, vc_ref, b_ref, o_ref, v1_ref, vc1_ref, *, rows, kh):
    exp2_scale = HEAD_DIM ** -0.5 * math.log2(math.e)
    nu = min(kh + 1, rows)
    nt = (((1,), (1,)), ((), ()))
    tn = (((0,), (0,)), ((), ()))
    kc = kc_ref[...]
    second = lax.broadcasted_iota(jnp.int32, (1, 2 * GRID_W), 1) >= GRID_W
    for src, dst in ((v_ref, v1_ref), (vc_ref, vc1_ref)):
        dst[:, :HEAD_DIM] = src[...]
        dst[:, HEAD_DIM:] = jnp.ones(src.shape, dst.dtype)
    vc1 = vc1_ref[...]

    def scores(p):
        i = 2 * p
        ustart = jnp.clip(i - kh // 2, 0, rows - nu)
        start0 = jnp.clip(i - kh // 2, 0, rows - kh)
        start1 = jnp.clip(i + 1 - kh // 2, 0, rows - kh)
        lo = jnp.where(second, start1, start0) - ustart
        d0 = ustart - i + (WIN_H - 1)
        qsl = pl.ds(pl.multiple_of(i * GRID_W, 2 * GRID_W), 2 * GRID_W)
        win = pl.ds(pl.multiple_of(ustart * GRID_W, GRID_W), nu * GRID_W)
        q2 = q_ref[qsl, :]
        bias = jnp.concatenate(
            [jnp.where((lo <= r) & (r < lo + kh), b_ref[d0 + r], MASK_VALUE) for r in range(nu)], axis=0)
        t_nb = lax.dot_general(k_ref[win, :], q2, nt, preferred_element_type=_F32) + bias
        t_cx = lax.dot_general(kc, q2, nt, preferred_element_type=_F32)
        return qsl, win, t_nb, t_cx

    def weights(t_nb, t_cx):
        m = jnp.maximum(jnp.max(t_nb, axis=0, keepdims=True), jnp.max(t_cx, axis=0, keepdims=True))
        return (jnp.exp2((t_nb - m) * exp2_scale).astype(_BF16), jnp.exp2((t_cx - m) * exp2_scale).astype(_BF16))

    def step(t, carry):
        staged = [scores(NATTN_PAIRS_PER_STEP * t + g) for g in range(NATTN_PAIRS_PER_STEP)]
        probs = [weights(t_nb, t_cx) for _, _, t_nb, t_cx in staged]
        for (qsl, win, _, _), (e_nb, e_cx) in zip(staged, probs):
            r = lax.dot_general(e_nb, v1_ref[win, :], tn, preferred_element_type=_F32)
            r = r + lax.dot_general(e_cx, vc1, tn, preferred_element_type=_F32)
            o_ref[qsl, :] = (r[:, :HEAD_DIM] / r[:, HEAD_DIM:]).astype(o_ref.dtype)
        return carry

    lax.fori_loop(0, rows // (2 * NATTN_PAIRS_PER_STEP), step, 0)


def _bias_table(rpb):
    j = jnp.arange(GRID_W)
    col_start = jnp.clip(j - WIN_W // 2, 0, GRID_W - WIN_W)
    valid = (j[None, :] >= col_start[:, None]) & (j[None, :] < col_start[:, None] + WIN_W)
    dc = jnp.clip(j[None, :] - j[:, None] + (WIN_W - 1), 0, 2 * WIN_W - 2)
    onehot = (dc[None] == jnp.arange(2 * WIN_W - 1)[:, None, None]).astype(_F32)
    t = jnp.einsum("hdc,cqk->hdqk", rpb, onehot, precision=lax.Precision.HIGHEST)
    t = jnp.where(valid[None, None], t * HEAD_DIM ** 0.5, MASK_VALUE)
    t = jnp.swapaxes(t, 2, 3)
    masked = jnp.full_like(t[:, :1], MASK_VALUE)
    first = jnp.concatenate([t, masked], axis=1)
    second = jnp.concatenate([masked, t], axis=1)
    return jnp.concatenate([first, second], axis=-1).astype(_F32)


def _nattn(qkv, kvc, kc_col, vc_col, bias, batch, seq, n_ctx, n_heads):
    rows = seq // GRID_W
    kh = min(WIN_H, rows)
    g = NATTN_HEADS_PER_STEP
    assert rows % (2 * NATTN_PAIRS_PER_STEP) == 0 and n_heads % g == 0
    a = n_heads * HEAD_DIM
    hb = lambda off: off // (g * HEAD_DIM)
    lat = lambda off: pl.BlockSpec((g, seq, HEAD_DIM), lambda b, h: (hb(off) + h, b, 0))
    cxt = lambda off: pl.BlockSpec((g, n_ctx, HEAD_DIM), lambda b, h: (hb(off) + h, b, 0))
    return pl.pallas_call(
        functools.partial(_nattn_kernel, rows=rows, kh=kh),
        out_shape=jax.ShapeDtypeStruct((batch * seq, a), _BF16),
        grid=(batch, n_heads // g),
        in_specs=[
            lat(0), lat(a), lat(2 * a), cxt(kc_col), cxt(vc_col),
            pl.BlockSpec((g, 2 * WIN_H, GRID_W, 2 * GRID_W), lambda b, h: (h, 0, 0, 0)),
        ],
        out_specs=pl.BlockSpec((seq, g * HEAD_DIM), lambda b, h: (b, h)),
        scratch_shapes=[pltpu.VMEM((g, seq, 2 * HEAD_DIM), _BF16), pltpu.VMEM((g, n_ctx, 2 * HEAD_DIM), _BF16)],
        compiler_params=_params("parallel", "parallel"),
        name="nattn",
    )(qkv, qkv, qkv, kvc, kvc, bias)


def _cattn_kernel(q_ref, k_ref, v_ref, o_ref):
    scale = HEAD_DIM ** -0.5
    s = lax.dot_general(q_ref[...], k_ref[...], (((1,), (1,)), ((), ())), preferred_element_type=_F32) * scale
    e = jnp.exp(s - jnp.max(s, axis=-1, keepdims=True))
    p = e * (1.0 / jnp.sum(e, axis=-1, keepdims=True))
    o_ref[...] = _dot(p.astype(_BF16), v_ref[...]).astype(o_ref.dtype)


def _cattn(qkv, batch, n_ctx, n_heads):
    a = n_heads * HEAD_DIM
    spec = lambda off: pl.BlockSpec((None, n_ctx, HEAD_DIM), lambda b, h: (off // HEAD_DIM + h, b, 0))
    return pl.pallas_call(
        _cattn_kernel,
        out_shape=jax.ShapeDtypeStruct((batch * n_ctx, a), _BF16),
        grid=(batch, n_heads),
        in_specs=[spec(0), spec(a), spec(2 * a)],
        out_specs=pl.BlockSpec((n_ctx, HEAD_DIM), lambda b, h: (b, h)),
        compiler_params=_params("parallel", "parallel"),
        name="cattn",
    )(qkv, qkv, qkv)


def _fill_padded(pad_ref, seq, fill_chunk, chunk):
    zeros = jnp.zeros((SEQ_PAD, pad_ref.shape[1]), pad_ref.dtype)
    pad_ref[pl.ds(0, SEQ_PAD), :] = zeros
    pad_ref[pl.ds(SEQ_PAD + seq, SEQ_PAD), :] = zeros

    def body(c, carry):
        t0 = pl.multiple_of(c * chunk, chunk)
        pad_ref[pl.ds(SEQ_PAD + t0, chunk), :] = fill_chunk(t0)
        return carry

    lax.fori_loop(0, seq // chunk, body, 0)


def _pool_kernel(u_ref, w_ref, s_ref, o_ref, pad_ref, *, seq, chunk):
    g = pl.program_id(1)
    _fill_padded(pad_ref, seq, lambda t0: u_ref[pl.ds(t0, chunk), :], chunk)
    w = w_ref[...]
    s = s_ref[...]

    for gi, win in enumerate(POOL_WINDOWS):

        @pl.when(g == gi)
        def _(win=win):
            def body(c, carry):
                t0 = pl.multiple_of(c * chunk, chunk)
                blk = pad_ref[pl.ds(t0, chunk + 2 * SEQ_PAD), :]
                run, step = blk, 1
                while step < win:
                    run = run + pltpu.roll(run, run.shape[0] - step, axis=0)
                    step *= 2
                tot = run[SEQ_PAD - win // 2:SEQ_PAD - win // 2 + chunk, :]
                t = t0 + lax.broadcasted_iota(jnp.int32, (chunk, blk.shape[1]), 0)
                lo = jnp.maximum(t - win // 2, 0)
                hi = jnp.minimum(t + (win - 1 - win // 2), seq - 1)
                cnt = (hi - lo + 1).astype(_F32)
                pooled = tot / cnt - blk[SEQ_PAD:SEQ_PAD + chunk, :]
                o_ref[pl.ds(t0, chunk), :] = (_dot(pooled.astype(_BF16), w) * s).astype(o_ref.dtype)
                return carry

            lax.fori_loop(0, seq // chunk, body, 0)


def _pool(u, pool_w, pool_scale, batch, seq):
    n_groups, c, _ = pool_w.shape
    chunk = min(128, seq)
    return pl.pallas_call(
        functools.partial(_pool_kernel, seq=seq, chunk=chunk),
        out_shape=jax.ShapeDtypeStruct((batch * seq, n_groups * c), _BF16),
        grid=(batch, n_groups),
        in_specs=[
            pl.BlockSpec((seq, c), lambda b, g: (b, g)),
            pl.BlockSpec((None, c, c), lambda b, g: (g, 0, 0)),
            pl.BlockSpec((1, c), lambda b, g: (0, g)),
        ],
        out_specs=pl.BlockSpec((seq, c), lambda b, g: (b, g)),
        scratch_shapes=[pltpu.VMEM((seq + 2 * SEQ_PAD, c), _F32)],
        compiler_params=_params("parallel", "parallel"),
        name="pool",
    )(u, pool_w, pool_scale.reshape(1, n_groups * c))


def _dwconv_kernel(a_ref, gate_ref, w_ref, b_ref, o_ref, pad_ref, wb_ref, *, seq, chunk):
    def glu(t0):
        return a_ref[pl.ds(t0, chunk), :] * jax.nn.sigmoid(gate_ref[pl.ds(t0, chunk), :])

    _fill_padded(pad_ref, seq, glu, chunk)
    n_ch = o_ref.shape[1]
    for k in range(CONV_KSIZE):
        wb_ref[pl.ds(V7X_SUBLANES * k, V7X_SUBLANES), :] = jnp.broadcast_to(w_ref[k:k + 1, :], (V7X_SUBLANES, n_ch))
    wb_ref[pl.ds(V7X_SUBLANES * CONV_KSIZE, V7X_SUBLANES), :] = jnp.broadcast_to(b_ref[...], (V7X_SUBLANES, n_ch))
    half = CONV_KSIZE // 2
    span = chunk + 2 * SEQ_PAD - V7X_SUBLANES
    groups = chunk // V7X_SUBLANES

    def tap(k):
        return wb_ref[pl.ds(V7X_SUBLANES * k, V7X_SUBLANES), :][None]

    def body(c, carry):
        t0 = pl.multiple_of(c * chunk, chunk)
        blk = pad_ref[pl.ds(t0, chunk + 2 * SEQ_PAD), :]
        acc = jnp.broadcast_to(tap(CONV_KSIZE), (groups, V7X_SUBLANES, n_ch))
        for res in range(V7X_SUBLANES):
            shifted = None
            for k in range(CONV_KSIZE):
                off = SEQ_PAD - half + k
                if off % V7X_SUBLANES != res:
                    continue
                if shifted is None:
                    shifted = blk[res:res + span, :]
                base = off - res
                acc = acc + shifted[base:base + chunk, :].reshape(groups, V7X_SUBLANES, n_ch) * tap(k)
        o_ref[pl.ds(t0, chunk), :] = acc.reshape(chunk, n_ch)
        return carry

    lax.fori_loop(0, seq // chunk, body, 0)


def _dwconv(u, a_col, gate_col, dw_w, dw_b, batch, seq):
    ksz, c = dw_w.shape
    tc = V7X_LANES
    chunk = min(128, seq)
    return pl.pallas_call(
        functools.partial(_dwconv_kernel, seq=seq, chunk=chunk),
        out_shape=jax.ShapeDtypeStruct((batch * seq, c), _F32),
        grid=(batch, c // tc),
        in_specs=[
            pl.BlockSpec((seq, tc), lambda b, j: (b, a_col // tc + j)),
            pl.BlockSpec((seq, tc), lambda b, j: (b, gate_col // tc + j)),
            pl.BlockSpec((ksz, tc), lambda b, j: (0, j)),
            pl.BlockSpec((1, tc), lambda b, j: (0, j)),
        ],
        out_specs=pl.BlockSpec((seq, tc), lambda b, j: (b, j)),
        scratch_shapes=[pltpu.VMEM((seq + 2 * SEQ_PAD, tc), _F32),
                        pltpu.VMEM((V7X_SUBLANES * (ksz + 1), tc), _F32)],
        compiler_params=_params("parallel", "parallel"),
        name="dwconv",
    )(u, u, dw_w, dw_b.reshape(1, c))


def _pw_kernel(h_ref, g_ref, w_ref, o_ref):
    x = h_ref[...]
    y = x * lax.rsqrt(jnp.mean(x * x, axis=-1, keepdims=True) + EPS) * g_ref[...]
    y = y * jax.nn.sigmoid(y)
    o_ref[...] = _dot(y.astype(_BF16), w_ref[...]).astype(o_ref.dtype)


def _pw(h, g, w, tm):
    m, c = h.shape
    return pl.pallas_call(
        _pw_kernel,
        out_shape=jax.ShapeDtypeStruct((m, c), _BF16),
        grid=(m // tm,),
        in_specs=[
            pl.BlockSpec((tm, c), lambda i: (i, 0)),
            pl.BlockSpec((1, c), lambda i: (0, 0)),
            pl.BlockSpec((c, c), lambda i: (0, 0)),
        ],
        out_specs=pl.BlockSpec((tm, c), lambda i: (i, 0)),
        compiler_params=_params("parallel"),
        name="conv_pw",
    )(h, g.reshape(1, c), w)


def _outproj_kernel(yp_ref, yc_ref, ya_ref, w_ref, x_ref, gate_ref, o_ref, *, side):
    kp = yp_ref.shape[1]
    kc = yc_ref.shape[1]
    yp, yc, ya = yp_ref[...], yc_ref[...], ya_ref[...]
    for sl in _col_chunks(o_ref.shape[1]):
        acc = _dot(yp, w_ref[pl.ds(0, kp), sl])
        acc = acc + _dot(yc, w_ref[pl.ds(kp, kc), sl])
        acc = acc + _dot(ya, w_ref[pl.ds(kp + kc, ya.shape[1]), sl])
        o_ref[:, sl] = x_ref[:, sl] + gate_ref[:, sl] * acc
    side()


def _outproj(yp, yc, ya, w, x, mod, gate_idx, rows, tm, tn, sides=()):
    m, d = x.shape
    tn = _tile(d, tn)
    lhs = lambda y: pl.BlockSpec((tm, y.shape[1]), lambda i, j: (i, 0))
    return _call_with_side_casts(
        _outproj_kernel, (yp, yc, ya, w, x, mod), sides,
        out_shape=jax.ShapeDtypeStruct((m, d), _F32),
        grid=(m // tm, d // tn),
        in_specs=[
            lhs(yp), lhs(yc), lhs(ya),
            pl.BlockSpec((w.shape[0], tn), lambda i, j: (0, j)),
            pl.BlockSpec((tm, tn), lambda i, j: (i, j)),
            _mod_spec(tn, gate_idx, rows, tm),
        ],
        out_specs=pl.BlockSpec((tm, tn), lambda i, j: (i, j)),
        name="outproj",
    )


def _mlp1_kernel(h_ref, w_ref, o_ref, *, side):
    h = h_ref[...]
    for sl in _col_chunks(o_ref.shape[1]):
        a = jnp.maximum(_dot(h, w_ref[:, sl]), 0.0)
        o_ref[:, sl] = (a * a).astype(o_ref.dtype)
    side()


def _mlp1(h, w, tm, tn, sides=()):
    m, d = h.shape
    f = w.shape[1]
    tn = _tile(f, tn)
    return _call_with_side_casts(
        _mlp1_kernel, (h, w), sides,
        out_shape=jax.ShapeDtypeStruct((m, f), _BF16),
        grid=(m // tm, f // tn),
        in_specs=[
            pl.BlockSpec((tm, d), lambda i, j: (i, 0)),
            pl.BlockSpec((d, tn), lambda i, j: (0, j)),
        ],
        out_specs=pl.BlockSpec((tm, tn), lambda i, j: (i, j)),
        name="mlp1",
    )


def _mlp2_kernel(a_ref, w_ref, x_ref, gate_ref, o_ref, *, nk, side):
    k = pl.program_id(2)
    chunks = _col_chunks(o_ref.shape[1])

    @pl.when(k == 0)
    def _():
        for sl in chunks:
            o_ref[:, sl] = _dot(a_ref[...], w_ref[:, sl])
        side()

    @pl.when(k > 0)
    def _():
        for sl in chunks:
            o_ref[:, sl] += _dot(a_ref[...], w_ref[:, sl])
        side()

    @pl.when(k == nk - 1)
    def _():
        o_ref[...] = x_ref[...] + gate_ref[...] * o_ref[...]


def _mlp2(a, w, x, mod, gate_idx, rows, tm, tn, tk, sides=()):
    m, d = x.shape
    f = a.shape[1]
    tn = _tile(d, tn)
    tk = _tile(f, tk)
    nk = f // tk
    return _call_with_side_casts(
        functools.partial(_mlp2_kernel, nk=nk), (a, w, x, mod), sides,
        out_shape=jax.ShapeDtypeStruct((m, d), _F32),
        grid=(m // tm, d // tn, nk),
        in_specs=[
            pl.BlockSpec((tm, tk), lambda i, j, k: (i, k)),
            pl.BlockSpec((tk, tn), lambda i, j, k: (k, j)),
            pl.BlockSpec((tm, tn), lambda i, j, k: (i, j)),
            _mod_spec(tn, gate_idx, rows, tm),
        ],
        out_specs=pl.BlockSpec((tm, tn), lambda i, j, k: (i, j)),
        name="mlp2",
        semantics=("parallel", "parallel", "arbitrary"),
        vmem_limit_bytes=V7X_VMEM_LIMIT_BIG_TILES_BYTES,
    )


def _mixers(u_pc, y_attn, p, batch, seq, tm):
    pool_width = p["pool_w"].shape[0] * p["pool_w"].shape[1]
    conv_width = p["conv_dw_w"].shape[1]
    y_pool = _pool(u_pc, p["pool_w"], p["pool_scale"], batch, seq)
    hconv = _dwconv(u_pc, pool_width, pool_width + conv_width, p["conv_dw_w"], p["conv_dw_b"], batch, seq)
    y_conv = _pw(hconv, p["conv_norm_g"], p["conv_pw_w"], tm)
    return y_pool, y_conv, y_attn


class _Bf16Weights:
    def __init__(self, raw, depth):
        self.raw, self.depth, self.done = raw, depth, {}

    def sides(self, *wanted):
        self.pending = [(n, l) for n, l in wanted if l < self.depth and (n, l) not in self.done]
        return [_SideCast(self.raw[n], l) for n, l in self.pending]

    def store(self, cast):
        self.done.update(zip(self.pending, cast))

    def __getitem__(self, key):
        return self.done[key]


def _post_attention(x, ys, p, bf, layer, mod, rows, tm, cast_ahead):
    ahead = (lambda *wanted: bf.sides(*wanted)) if cast_ahead else (lambda *wanted: ())
    x, cast = _outproj(*ys, bf["w_out", layer], x, mod, 2, rows, tm, OUTPROJ_COLS, sides=ahead(("w_mlp1", layer)))
    if cast_ahead:
        bf.store(cast)
    h = _normmod(x, p["norm2_g"], mod, 3, rows)
    a, cast = _mlp1(h, bf["w_mlp1", layer], tm, MLP1_COLS, sides=ahead(("w_mlp2", layer), ("w_in", layer + 1)))
    if cast_ahead:
        bf.store(cast)
    x, cast = _mlp2(a, bf["w_mlp2", layer], x, mod, 5, rows, tm, MLP2_COLS, MLP2_DEPTH,
                    sides=ahead(("w_mlp1", layer + 1), ("w_out", layer + 1)))
    if cast_ahead:
        bf.store(cast)
    return x


def kernel(x, c, ctx, c_ctx, w_ada, b_ada, norm1_g, norm2_g, w_in, pool_w, pool_scale, conv_dw_w, conv_dw_b,
           conv_norm_g, conv_pw_w, q_norm_g, k_norm_g, rpb, w_out, w_mlp1, w_mlp2):
    batch, seq, d = x.shape
    n_ctx = ctx.shape[1]
    depth = w_in.shape[0]
    n_heads = rpb.shape[1]
    attn_w = n_heads * HEAD_DIM
    pool_width = pool_w.shape[1] * pool_w.shape[2]
    conv_width = conv_dw_w.shape[2]
    off_q = pool_width + 2 * conv_width
    off_k = off_q + attn_w
    kh = min(WIN_H, seq // GRID_W)

    m_lat, m_ctx = batch * seq, batch * n_ctx
    tm_lat, tm_ctx = _tile(seq, MATMUL_ROWS), _tile(m_ctx, MATMUL_ROWS)
    ctx_row = batch
    lat_row = _Rows(base=0, tokens=seq)
    cx_row = _Rows(base=ctx_row, tokens=m_ctx)

    x = x.reshape(m_lat, d)
    xc = ctx.reshape(m_ctx, d)
    n_rows = -(-(batch + 1) // V7X_SUBLANES) * V7X_SUBLANES
    cvec = jnp.zeros((n_rows, d), _F32).at[:batch].set(c).at[ctx_row].set(c_ctx)
    qk_gain = jnp.concatenate([jnp.tile(q_norm_g, (1, n_heads)), jnp.tile(k_norm_g, (1, n_heads))], axis=1)

    bf = _Bf16Weights({"w_in": w_in, "w_out": w_out, "w_mlp1": w_mlp1, "w_mlp2": w_mlp2}, depth)
    bf.done["w_in", 0] = _to_bf16(w_in, 0)
    for layer in range(depth):
        w_in_bf = bf["w_in", layer]
        p = {
            "pool_w": _to_bf16(pool_w.reshape(depth, -1, pool_w.shape[-1]), layer).reshape(pool_w.shape[1:]),
            "pool_scale": pool_scale[layer],
            "conv_dw_w": conv_dw_w[layer], "conv_dw_b": conv_dw_b[layer],
            "conv_norm_g": conv_norm_g[layer], "conv_pw_w": _to_bf16(conv_pw_w, layer),
            "norm2_g": norm2_g[layer],
        }
        gains = qk_gain[layer:layer + 1]
        update_ctx = layer < depth - 1
        mod = _ada(cvec, w_ada, b_ada, layer).reshape(n_rows, N_MOD, 1, d)
        bias = _bias_table(rpb[layer])

        hc = _normmod(xc, norm1_g[layer], mod, 0, cx_row)
        if update_ctx:
            uc_pc, _ = _proj(hc, w_in_bf, 0, off_q, None, _F32, tm_ctx, PROJ_COLS, "proj_pc")
            qkv_c, _ = _proj(hc, w_in_bf, off_q, 3 * attn_w, gains, _BF16, tm_ctx, PROJ_COLS, "proj_qkv")
            kc_col, vc_col = attn_w, 2 * attn_w
        else:
            qkv_c, _ = _proj(hc, w_in_bf, off_k, 2 * attn_w, gains[:, attn_w:], _BF16, tm_ctx, PROJ_COLS, "proj_kv")
            kc_col, vc_col = 0, attn_w

        h = _normmod(x, norm1_g[layer], mod, 0, lat_row)
        u_pc, _ = _proj(h, w_in_bf, 0, off_q, None, _F32, tm_lat, PROJ_COLS, "proj_pc")
        qkv, cast = _proj(h, w_in_bf, off_q, 3 * attn_w, gains, _BF16, tm_lat, PROJ_COLS, "proj_qkv",
                          sides=bf.sides(("w_out", layer)))
        bf.store(cast)
        y_attn = _nattn(qkv, qkv_c, kc_col, vc_col, bias, batch, seq, n_ctx, n_heads)
        x = _post_attention(x, _mixers(u_pc, y_attn, p, batch, seq, tm_lat), p, bf, layer, mod, lat_row, tm_lat,
                            cast_ahead=True)

        if update_ctx:
            yc_attn = _cattn(qkv_c, batch, n_ctx, n_heads)
            xc = _post_attention(xc, _mixers(uc_pc, yc_attn, p, batch, n_ctx, tm_ctx), p, bf, layer, mod, cx_row,
                                 tm_ctx, cast_ahead=False)

    return x.reshape(batch, seq, d)
```

```python
import functools
import math
from typing import Callable, NamedTuple

import jax
import jax.numpy as jnp
from jax import lax
from jax.experimental import pallas as pl
from jax.experimental.pallas import tpu as pltpu

GRID_W = 64
WIN_H = 8
WIN_W = 16
HEAD_DIM = 128
POOL_WINDOWS = (2, 4, 8, 16)
CONV_KSIZE = 31
N_MOD = 6
EPS = 1e-6
MASK_VALUE = -1e30

V7X_LANES = 128
V7X_SUBLANES = 8
V7X_MXU_COLS = 256
V7X_VMEM_LIMIT_BYTES = 56 * 1024 * 1024
V7X_VMEM_LIMIT_BIG_TILES_BYTES = 60 * 1024 * 1024
MATMUL_ROWS = 1024
PROJ_COLS = 1024
OUTPROJ_COLS = 512
MLP1_COLS = 1024
MLP2_COLS = 1024
MLP2_DEPTH = 4096
NATTN_PAIRS_PER_STEP = 8
NATTN_HEADS_PER_STEP = 4
NORMMOD_ROWS = 512
SEQ_PAD = 16

_BF16 = jnp.bfloat16
_F32 = jnp.float32


def _params(*sem):
    return pltpu.CompilerParams(dimension_semantics=sem, vmem_limit_bytes=V7X_VMEM_LIMIT_BYTES)


def _tile(n, pref, *also):
    if n <= pref and not any(a % n for a in also):
        return n
    t = min(pref, n) - min(pref, n) % V7X_LANES
    while n % t or any(a % t for a in also):
        t -= V7X_LANES
    return t


def _dot(a, b):
    return jnp.dot(a, b, preferred_element_type=_F32)


def _cast_kernel(w_ref, o_ref):
    o_ref[...] = w_ref[...].astype(o_ref.dtype)


def _to_bf16(w, layer):
    _, r, c = w.shape
    tr, tc = _tile(r, 512), _tile(c, 2048)
    return pl.pallas_call(
        _cast_kernel,
        out_shape=jax.ShapeDtypeStruct((r, c), _BF16),
        grid=(r // tr, c // tc),
        in_specs=[pl.BlockSpec((None, tr, tc), lambda i, j: (layer, i, j))],
        out_specs=pl.BlockSpec((tr, tc), lambda i, j: (i, j)),
        compiler_params=_params("parallel", "parallel"),
        name="cast_bf16",
    )(w)


class _SideCast(NamedTuple):
    w: jax.Array
    layer: int


def _linear_step(idx, grid):
    step = idx[0]
    for i, n in zip(idx[1:], grid[1:]):
        step = step * n + i
    return step


class _SideAda(NamedTuple):
    cvec: jax.Array
    w_ada: jax.Array
    b_ada: jax.Array
    layer: int


class _SidePlan(NamedTuple):
    n_blocks: int
    args: tuple
    in_specs: tuple
    out_shape: jax.ShapeDtypeStruct
    out_spec: pl.BlockSpec
    compute: Callable


def _cast_compute(src, dst):
    dst[...] = src[...].astype(dst.dtype)


def _side_plan(side, grid):
    n_steps = math.prod(grid)
    if isinstance(side, _SideCast):
        _, r, c = side.w.shape
        bf16_rows = 2 * V7X_SUBLANES
        nb = max(b for b in range(1, n_steps + 1) if r % b == 0 and (r // b) % bf16_rows == 0)
    else:
        r, d = side.cvec.shape
        n = side.w_ada.shape[2]
        tn = min(t for t in range(V7X_MXU_COLS, n + 1, V7X_MXU_COLS) if n % t == 0 and n // t <= n_steps)
        nb = n // tn

    def blk(*idx):
        return jnp.minimum(_linear_step(idx, grid), nb - 1)

    if isinstance(side, _SideCast):
        tr = r // nb
        return _SidePlan(nb, (side.w,),
                         (pl.BlockSpec((None, tr, c), lambda *idx: (side.layer, blk(*idx), 0)),),
                         jax.ShapeDtypeStruct((r, c), _BF16),
                         pl.BlockSpec((tr, c), lambda *idx: (blk(*idx), 0)),
                         _cast_compute)
    return _SidePlan(nb, (side.cvec, side.w_ada, side.b_ada),
                     (pl.BlockSpec((r, d), lambda *idx: (0, 0)),
                      pl.BlockSpec((None, d, tn), lambda *idx: (side.layer, 0, blk(*idx))),
                      pl.BlockSpec((None, 1, tn), lambda *idx: (side.layer, 0, blk(*idx)))),
                     jax.ShapeDtypeStruct((r, n), _F32),
                     pl.BlockSpec((r, tn), lambda *idx: (0, blk(*idx))),
                     _ada_kernel)


def _call_with_side_casts(body, args, sides, *, out_shape, grid, in_specs, out_specs, name, semantics=None,
                          vmem_limit_bytes=V7X_VMEM_LIMIT_BYTES):
    plans = [_side_plan(s, grid) for s in sides]
    n_in = len(args)
    n_side_in = sum(len(p.args) for p in plans)

    def wrapped(*refs):
        side_out = refs[n_in + n_side_in + 1:]

        def side():
            step = _linear_step([pl.program_id(a) for a in range(len(grid))], grid)
            first = n_in
            for p, dst in zip(plans, side_out):
                srcs = refs[first:first + len(p.args)]
                first += len(p.args)
                if p.n_blocks == math.prod(grid):
                    p.compute(*srcs, dst)
                else:
                    pl.when(step < p.n_blocks)(functools.partial(p.compute, *srcs, dst))

        body(*refs[:n_in], refs[n_in + n_side_in], side=side)

    if semantics is None or sides:
        semantics = ("arbitrary" if sides else "parallel",) * len(grid)
    outs = pl.pallas_call(
        wrapped,
        out_shape=[out_shape] + [p.out_shape for p in plans],
        grid=grid,
        in_specs=list(in_specs) + [spec for p in plans for spec in p.in_specs],
        out_specs=[out_specs] + [p.out_spec for p in plans],
        compiler_params=pltpu.CompilerParams(dimension_semantics=semantics, vmem_limit_bytes=vmem_limit_bytes),
        name=name,
    )(*args, *[a for p in plans for a in p.args])
    return outs[0], list(outs[1:])


def _ada_kernel(c_ref, w_ref, b_ref, o_ref):
    c = c_ref[...]
    s = (c * jax.nn.sigmoid(c)).astype(_BF16)
    o_ref[...] = _dot(s, w_ref[...].astype(_BF16)) + b_ref[...]


def _ada(cvec, w_ada, b_ada, layer):
    r, d = cvec.shape
    n = w_ada.shape[2]
    tn = _tile(n, 1024)
    return pl.pallas_call(
        _ada_kernel,
        out_shape=jax.ShapeDtypeStruct((r, n), _F32),
        grid=(n // tn,),
        in_specs=[
            pl.BlockSpec((r, d), lambda j: (0, 0)),
            pl.BlockSpec((None, d, tn), lambda j: (layer, 0, j)),
            pl.BlockSpec((None, 1, tn), lambda j: (layer, 0, j)),
        ],
        out_specs=pl.BlockSpec((r, tn), lambda j: (0, j)),
        compiler_params=_params("parallel"),
        name="ada",
    )(cvec, w_ada, b_ada.reshape(b_ada.shape[0], 1, n))


class _Rows(NamedTuple):
    base: int
    tokens: int

    def of_block(self, i, tm):
        return self.base + (i * tm) // self.tokens


def _mod_spec(tn, which, rows, tm):
    return pl.BlockSpec((None, None, 1, tn), lambda i, j=0, *_: (rows.of_block(i, tm), which, 0, j))


def _normmod_kernel(x_ref, g_ref, shift_ref, scale_ref, o_ref):
    group = 2 * V7X_SUBLANES
    gain = g_ref[...]
    one_plus_scale = 1.0 + scale_ref[...]
    shift = shift_ref[...]

    def body(r, carry):
        rs = pl.ds(pl.multiple_of(r * group, group), group)
        x = x_ref[rs, :]
        y = x * lax.rsqrt(jnp.mean(x * x, axis=-1, keepdims=True) + EPS) * gain
        o_ref[rs, :] = (y * one_plus_scale + shift).astype(o_ref.dtype)
        return carry

    lax.fori_loop(0, x_ref.shape[0] // group, body, 0, unroll=4)


def _normmod(x, g, mod, shift_idx, rows):
    m, d = x.shape
    tm = _tile(rows.tokens, NORMMOD_ROWS)
    return pl.pallas_call(
        _normmod_kernel,
        out_shape=jax.ShapeDtypeStruct((m, d), _BF16),
        grid=(m // tm,),
        in_specs=[
            pl.BlockSpec((tm, d), lambda i: (i, 0)),
            pl.BlockSpec((1, d), lambda i: (0, 0)),
            _mod_spec(d, shift_idx, rows, tm),
            _mod_spec(d, shift_idx + 1, rows, tm),
        ],
        out_specs=pl.BlockSpec((tm, d), lambda i: (i, 0)),
        compiler_params=_params("parallel"),
        name="normmod",
    )(x, g.reshape(1, d), mod, mod)


def _col_chunks(n):
    c = V7X_MXU_COLS if n % V7X_MXU_COLS == 0 else n
    return [slice(s, s + c) for s in range(0, n, c)]


def _proj_plain_kernel(h_ref, w_ref, o_ref, *, side):
    h = h_ref[...]
    for sl in _col_chunks(o_ref.shape[1]):
        o_ref[:, sl] = _dot(h, w_ref[:, sl]).astype(o_ref.dtype)
    side()


def _proj_kernel(h_ref, w_ref, g_ref, o_ref, *, n_norm_blocks, side):
    normed = pl.program_id(1) < n_norm_blocks
    h = h_ref[...]
    for sl in _col_chunks(w_ref.shape[1]):
        acc = _dot(h, w_ref[:, sl])
        for c in range(0, acc.shape[1], HEAD_DIM):
            a = acc[:, c:c + HEAD_DIM]
            rstd = lax.rsqrt(jnp.mean(a * a, axis=-1, keepdims=True) + EPS)
            gain = g_ref[:, sl.start + c:sl.start + c + HEAD_DIM]
            y = a * jnp.where(normed, rstd, 1.0) * jnp.where(normed, gain, 1.0)
            o_ref[(sl.start + c) // HEAD_DIM] = y.astype(o_ref.dtype)
    side()


def _proj(h, w, col0, ncols, gains, out_dtype, tm, tn, name, sides=()):
    m, d = h.shape
    tn = _tile(ncols, tn, col0, *(() if gains is None else (gains.shape[1],)))
    cb = col0 // tn
    in_specs = [
        pl.BlockSpec((tm, d), lambda i, j: (i, 0)),
        pl.BlockSpec((d, tn), lambda i, j: (0, cb + j)),
    ]
    if gains is None:
        body, args = _proj_plain_kernel, (h, w)
        out_shape = jax.ShapeDtypeStruct((m, ncols), out_dtype)
        out_spec = pl.BlockSpec((tm, tn), lambda i, j: (i, j))
    else:
        n_norm_blocks = gains.shape[1] // tn
        body, args = functools.partial(_proj_kernel, n_norm_blocks=n_norm_blocks), (h, w, gains)
        in_specs.append(pl.BlockSpec((1, tn), lambda i, j: (0, jnp.minimum(j, n_norm_blocks - 1))))
        out_shape = jax.ShapeDtypeStruct((ncols // HEAD_DIM, m, HEAD_DIM), out_dtype)
        out_spec = pl.BlockSpec((tn // HEAD_DIM, tm, HEAD_DIM), lambda i, j: (j, i, 0))
    return _call_with_side_casts(
        body, args, sides,
        out_shape=out_shape,
        grid=(m // tm, ncols // tn),
        in_specs=in_specs,
        out_specs=out_spec,
        name=name,
    )


def _nattn_kernel(q_ref, k_ref, v_ref, kc_ref, vc_ref, b_ref, o_ref, v1_ref, vc1_ref, *, rows, kh):
    for g in range(q_ref.shape[0]):
        _nattn_head(q_ref.at[g], k_ref.at[g], v_ref.at[g], kc_ref.at[g], vc_ref.at[g], b_ref.at[g],
                    o_ref.at[:, g * HEAD_DIM:(g + 1) * HEAD_DIM], v1_ref.at[g], vc1_ref.at[g], rows=rows, kh=kh)


def _nattn_head(q_ref, k_ref, v_ref, kc_ref, vc_ref, b_ref, o_ref, v1_ref, vc1_ref, *, rows, kh):
    exp2_scale = HEAD_DIM ** -0.5 * math.log2(math.e)
    nu = min(kh + 1, rows)
    nt = (((1,), (1,)), ((), ()))
    tn = (((0,), (0,)), ((), ()))
    kc = kc_ref[...]
    second = lax.broadcasted_iota(jnp.int32, (1, 2 * GRID_W), 1) >= GRID_W
    for src, dst in ((v_ref, v1_ref), (vc_ref, vc1_ref)):
        dst[:, :HEAD_DIM] = src[...]
        dst[:, HEAD_DIM:] = jnp.ones(src.shape, dst.dtype)
    vc1 = vc1_ref[...]

    def scores(p):
        i = 2 * p
        ustart = jnp.clip(i - kh // 2, 0, rows - nu)
        start0 = jnp.clip(i - kh // 2, 0, rows - kh)
        start1 = jnp.clip(i + 1 - kh // 2, 0, rows - kh)
        lo = jnp.where(second, start1, start0) - ustart
        d0 = ustart - i + (WIN_H - 1)
        qsl = pl.ds(pl.multiple_of(i * GRID_W, 2 * GRID_W), 2 * GRID_W)
        win = pl.ds(pl.multiple_of(ustart * GRID_W, GRID_W), nu * GRID_W)
        q2 = q_ref[qsl, :]
        bias = jnp.concatenate(
            [jnp.where((lo <= r) & (r < lo + kh), b_ref[d0 + r], MASK_VALUE) for r in range(nu)], axis=0)
        t_nb = lax.dot_general(k_ref[win, :], q2, nt, preferred_element_type=_F32) + bias
        t_cx = lax.dot_general(kc, q2, nt, preferred_element_type=_F32)
        return qsl, win, t_nb, t_cx

    def weights(t_nb, t_cx):
        m = jnp.maximum(jnp.max(t_nb, axis=0, keepdims=True), jnp.max(t_cx, axis=0, keepdims=True))
        return (jnp.exp2((t_nb - m) * exp2_scale).astype(_BF16), jnp.exp2((t_cx - m) * exp2_scale).astype(_BF16))

    def step(t, carry):
        staged = [scores(NATTN_PAIRS_PER_STEP * t + g) for g in range(NATTN_PAIRS_PER_STEP)]
        probs = [weights(t_nb, t_cx) for _, _, t_nb, t_cx in staged]
        for (qsl, win, _, _), (e_nb, e_cx) in zip(staged, probs):
            r = lax.dot_general(e_nb, v1_ref[win, :], tn, preferred_element_type=_F32)
            r = r + lax.dot_general(e_cx, vc1, tn, preferred_element_type=_F32)
            o_ref[qsl, :] = (r[:, :HEAD_DIM] / r[:, HEAD_DIM:]).astype(o_ref.dtype)
        return carry

    lax.fori_loop(0, rows // (2 * NATTN_PAIRS_PER_STEP), step, 0)


def _bias_table(rpb):
    j = jnp.arange(GRID_W)
    col_start = jnp.clip(j - WIN_W // 2, 0, GRID_W - WIN_W)
    valid = (j[None, :] >= col_start[:, None]) & (j[None, :] < col_start[:, None] + WIN_W)
    dc = jnp.clip(j[None, :] - j[:, None] + (WIN_W - 1), 0, 2 * WIN_W - 2)
    onehot = (dc[None] == jnp.arange(2 * WIN_W - 1)[:, None, None]).astype(_F32)
    t = jnp.einsum("hdc,cqk->hdqk", rpb, onehot, precision=lax.Precision.HIGHEST)
    t = jnp.where(valid[None, None], t * HEAD_DIM ** 0.5, MASK_VALUE)
    t = jnp.swapaxes(t, 2, 3)
    masked = jnp.full_like(t[:, :1], MASK_VALUE)
    first = jnp.concatenate([t, masked], axis=1)
    second = jnp.concatenate([masked, t], axis=1)
    return jnp.concatenate([first, second], axis=-1).astype(_F32)


def _nattn(qkv, kvc, kc_col, vc_col, bias, batch, seq, n_ctx, n_heads):
    rows = seq // GRID_W
    kh = min(WIN_H, rows)
    g = NATTN_HEADS_PER_STEP
    assert rows % (2 * NATTN_PAIRS_PER_STEP) == 0 and n_heads % g == 0
    a = n_heads * HEAD_DIM
    hb = lambda off: off // (g * HEAD_DIM)
    lat = lambda off: pl.BlockSpec((g, seq, HEAD_DIM), lambda b, h: (hb(off) + h, b, 0))
    cxt = lambda off: pl.BlockSpec((g, n_ctx, HEAD_DIM), lambda b, h: (hb(off) + h, b, 0))
    return pl.pallas_call(
        functools.partial(_nattn_kernel, rows=rows, kh=kh),
        out_shape=jax.ShapeDtypeStruct((batch * seq, a), _BF16),
        grid=(batch, n_heads // g),
        in_specs=[
            lat(0), lat(a), lat(2 * a), cxt(kc_col), cxt(vc_col),
            pl.BlockSpec((g, 2 * WIN_H, GRID_W, 2 * GRID_W), lambda b, h: (h, 0, 0, 0)),
        ],
        out_specs=pl.BlockSpec((seq, g * HEAD_DIM), lambda b, h: (b, h)),
        scratch_shapes=[pltpu.VMEM((g, seq, 2 * HEAD_DIM), _BF16), pltpu.VMEM((g, n_ctx, 2 * HEAD_DIM), _BF16)],
        compiler_params=_params("parallel", "parallel"),
        name="nattn",
    )(qkv, qkv, qkv, kvc, kvc, bias)


def _cattn_kernel(q_ref, k_ref, v_ref, o_ref):
    scale = HEAD_DIM ** -0.5
    s = lax.dot_general(q_ref[...], k_ref[...], (((1,), (1,)), ((), ())), preferred_element_type=_F32) * scale
    e = jnp.exp(s - jnp.max(s, axis=-1, keepdims=True))
    p = e * (1.0 / jnp.sum(e, axis=-1, keepdims=True))
    o_ref[...] = _dot(p.astype(_BF16), v_ref[...]).astype(o_ref.dtype)


def _cattn(qkv, batch, n_ctx, n_heads):
    a = n_heads * HEAD_DIM
    spec = lambda off: pl.BlockSpec((None, n_ctx, HEAD_DIM), lambda b, h: (off // HEAD_DIM + h, b, 0))
    return pl.pallas_call(
        _cattn_kernel,
        out_shape=jax.ShapeDtypeStruct((batch * n_ctx, a), _BF16),
        grid=(batch, n_heads),
        in_specs=[spec(0), spec(a), spec(2 * a)],
        out_specs=pl.BlockSpec((n_ctx, HEAD_DIM), lambda b, h: (b, h)),
        compiler_params=_params("parallel", "parallel"),
        name="cattn",
    )(qkv, qkv, qkv)


def _fill_padded(pad_ref, seq, fill_chunk, chunk):
    zeros = jnp.zeros((SEQ_PAD, pad_ref.shape[1]), pad_ref.dtype)
    pad_ref[pl.ds(0, SEQ_PAD), :] = zeros
    pad_ref[pl.ds(SEQ_PAD + seq, SEQ_PAD), :] = zeros

    def body(c, carry):
        t0 = pl.multiple_of(c * chunk, chunk)
        pad_ref[pl.ds(SEQ_PAD + t0, chunk), :] = fill_chunk(t0)
        return carry

    lax.fori_loop(0, seq // chunk, body, 0)


def _pool_kernel(u_ref, w_ref, s_ref, o_ref, pad_ref, *, seq, chunk):
    g = pl.program_id(1)
    _fill_padded(pad_ref, seq, lambda t0: u_ref[pl.ds(t0, chunk), :], chunk)
    w = w_ref[...]
    s = s_ref[...]

    for gi, win in enumerate(POOL_WINDOWS):

        @pl.when(g == gi)
        def _(win=win):
            def body(c, carry):
                t0 = pl.multiple_of(c * chunk, chunk)
                blk = pad_ref[pl.ds(t0, chunk + 2 * SEQ_PAD), :]
                run, step = blk, 1
                while step < win:
                    run = run + pltpu.roll(run, run.shape[0] - step, axis=0)
                    step *= 2
                tot = run[SEQ_PAD - win // 2:SEQ_PAD - win // 2 + chunk, :]
                t = t0 + lax.broadcasted_iota(jnp.int32, (chunk, blk.shape[1]), 0)
                lo = jnp.maximum(t - win // 2, 0)
                hi = jnp.minimum(t + (win - 1 - win // 2), seq - 1)
                cnt = (hi - lo + 1).astype(_F32)
                pooled = tot / cnt - blk[SEQ_PAD:SEQ_PAD + chunk, :]
                o_ref[pl.ds(t0, chunk), :] = (_dot(pooled.astype(_BF16), w) * s).astype(o_ref.dtype)
                return carry

            lax.fori_loop(0, seq // chunk, body, 0)


def _pool(u, pool_w, pool_scale, batch, seq):
    n_groups, c, _ = pool_w.shape
    chunk = min(128, seq)
    return pl.pallas_call(
        functools.partial(_pool_kernel, seq=seq, chunk=chunk),
        out_shape=jax.ShapeDtypeStruct((batch * seq, n_groups * c), _BF16),
        grid=(batch, n_groups),
        in_specs=[
            pl.BlockSpec((seq, c), lambda b, g: (b, g)),
            pl.BlockSpec((None, c, c), lambda b, g: (g, 0, 0)),
            pl.BlockSpec((1, c), lambda b, g: (0, g)),
        ],
        out_specs=pl.BlockSpec((seq, c), lambda b, g: (b, g)),
        scratch_shapes=[pltpu.VMEM((seq + 2 * SEQ_PAD, c), _F32)],
        compiler_params=_params("parallel", "parallel"),
        name="pool",
    )(u, pool_w, pool_scale.reshape(1, n_groups * c))


def _dwconv_kernel(a_ref, gate_ref, w_ref, b_ref, o_ref, pad_ref, wb_ref, *, seq, chunk):
    def glu(t0):
        return a_ref[pl.ds(t0, chunk), :] * jax.nn.sigmoid(gate_ref[pl.ds(t0, chunk), :])

    _fill_padded(pad_ref, seq, glu, chunk)
    n_ch = o_ref.shape[1]
    for k in range(CONV_KSIZE):
        wb_ref[pl.ds(V7X_SUBLANES * k, V7X_SUBLANES), :] = jnp.broadcast_to(w_ref[k:k + 1, :], (V7X_SUBLANES, n_ch))
    wb_ref[pl.ds(V7X_SUBLANES * CONV_KSIZE, V7X_SUBLANES), :] = jnp.broadcast_to(b_ref[...], (V7X_SUBLANES, n_ch))
    half = CONV_KSIZE // 2
    span = chunk + 2 * SEQ_PAD - V7X_SUBLANES
    groups = chunk // V7X_SUBLANES

    def tap(k):
        return wb_ref[pl.ds(V7X_SUBLANES * k, V7X_SUBLANES), :][None]

    def body(c, carry):
        t0 = pl.multiple_of(c * chunk, chunk)
        blk = pad_ref[pl.ds(t0, chunk + 2 * SEQ_PAD), :]
        acc = jnp.broadcast_to(tap(CONV_KSIZE), (groups, V7X_SUBLANES, n_ch))
        for res in range(V7X_SUBLANES):
            shifted = None
            for k in range(CONV_KSIZE):
                off = SEQ_PAD - half + k
                if off % V7X_SUBLANES != res:
                    continue
                if shifted is None:
                    shifted = blk[res:res + span, :]
                base = off - res
                acc = acc + shifted[base:base + chunk, :].reshape(groups, V7X_SUBLANES, n_ch) * tap(k)
        o_ref[pl.ds(t0, chunk), :] = acc.reshape(chunk, n_ch)
        return carry

    lax.fori_loop(0, seq // chunk, body, 0)


def _dwconv(u, a_col, gate_col, dw_w, dw_b, batch, seq):
    ksz, c = dw_w.shape
    tc = V7X_LANES
    chunk = min(128, seq)
    return pl.pallas_call(
        functools.partial(_dwconv_kernel, seq=seq, chunk=chunk),
        out_shape=jax.ShapeDtypeStruct((batch * seq, c), _F32),
        grid=(batch, c // tc),
        in_specs=[
            pl.BlockSpec((seq, tc), lambda b, j: (b, a_col // tc + j)),
            pl.BlockSpec((seq, tc), lambda b, j: (b, gate_col // tc + j)),
            pl.BlockSpec((ksz, tc), lambda b, j: (0, j)),
            pl.BlockSpec((1, tc), lambda b, j: (0, j)),
        ],
        out_specs=pl.BlockSpec((seq, tc), lambda b, j: (b, j)),
        scratch_shapes=[pltpu.VMEM((seq + 2 * SEQ_PAD, tc), _F32),
                        pltpu.VMEM((V7X_SUBLANES * (ksz + 1), tc), _F32)],
        compiler_params=_params("parallel", "parallel"),
        name="dwconv",
    )(u, u, dw_w, dw_b.reshape(1, c))


def _pw_kernel(h_ref, g_ref, w_ref, o_ref):
    x = h_ref[...]
    y = x * lax.rsqrt(jnp.mean(x * x, axis=-1, keepdims=True) + EPS) * g_ref[...]
    y = y * jax.nn.sigmoid(y)
    o_ref[...] = _dot(y.astype(_BF16), w_ref[...]).astype(o_ref.dtype)


def _pw(h, g, w, tm):
    m, c = h.shape
    return pl.pallas_call(
        _pw_kernel,
        out_shape=jax.ShapeDtypeStruct((m, c), _BF16),
        grid=(m // tm,),
        in_specs=[
            pl.BlockSpec((tm, c), lambda i: (i, 0)),
            pl.BlockSpec((1, c), lambda i: (0, 0)),
            pl.BlockSpec((c, c), lambda i: (0, 0)),
        ],
        out_specs=pl.BlockSpec((tm, c), lambda i: (i, 0)),
        compiler_params=_params("parallel"),
        name="conv_pw",
    )(h, g.reshape(1, c), w)


def _outproj_kernel(yp_ref, yc_ref, ya_ref, w_ref, x_ref, gate_ref, o_ref, *, side):
    kp = yp_ref.shape[1]
    kc = yc_ref.shape[1]
    yp, yc, ya = yp_ref[...], yc_ref[...], ya_ref[...]
    for sl in _col_chunks(o_ref.shape[1]):
        acc = _dot(yp, w_ref[pl.ds(0, kp), sl])
        acc = acc + _dot(yc, w_ref[pl.ds(kp, kc), sl])
        acc = acc + _dot(ya, w_ref[pl.ds(kp + kc, ya.shape[1]), sl])
        o_ref[:, sl] = x_ref[:, sl] + gate_ref[:, sl] * acc
    side()


def _outproj(yp, yc, ya, w, x, mod, gate_idx, rows, tm, tn, sides=()):
    m, d = x.shape
    tn = _tile(d, tn)
    lhs = lambda y: pl.BlockSpec((tm, y.shape[1]), lambda i, j: (i, 0))
    return _call_with_side_casts(
        _outproj_kernel, (yp, yc, ya, w, x, mod), sides,
        out_shape=jax.ShapeDtypeStruct((m, d), _F32),
        grid=(m // tm, d // tn),
        in_specs=[
            lhs(yp), lhs(yc), lhs(ya),
            pl.BlockSpec((w.shape[0], tn), lambda i, j: (0, j)),
            pl.BlockSpec((tm, tn), lambda i, j: (i, j)),
            _mod_spec(tn, gate_idx, rows, tm),
        ],
        out_specs=pl.BlockSpec((tm, tn), lambda i, j: (i, j)),
        name="outproj",
    )


def _mlp1_kernel(h_ref, w_ref, o_ref, *, side):
    for sl in _col_chunks(o_ref.shape[1]):
        a = jnp.maximum(_dot(h_ref[...], w_ref[:, sl]), 0.0)
        o_ref[:, sl] = (a * a).astype(o_ref.dtype)
    side()


def _mlp1(h, w, tm, tn, sides=()):
    m, d = h.shape
    f = w.shape[1]
    tn = _tile(f, tn)
    return _call_with_side_casts(
        _mlp1_kernel, (h, w), sides,
        out_shape=jax.ShapeDtypeStruct((m, f), _BF16),
        grid=(m // tm, f // tn),
        in_specs=[
            pl.BlockSpec((tm, d), lambda i, j: (i, 0)),
            pl.BlockSpec((d, tn), lambda i, j: (0, j)),
        ],
        out_specs=pl.BlockSpec((tm, tn), lambda i, j: (i, j)),
        name="mlp1",
    )


def _mlp2_kernel(a_ref, w_ref, x_ref, gate_ref, o_ref, *, nk, side):
    k = pl.program_id(2)
    chunks = _col_chunks(o_ref.shape[1])

    @pl.when(k == 0)
    def _():
        for sl in chunks:
            o_ref[:, sl] = _dot(a_ref[...], w_ref[:, sl])
        side()

    @pl.when(k > 0)
    def _():
        for sl in chunks:
            o_ref[:, sl] += _dot(a_ref[...], w_ref[:, sl])
        side()

    @pl.when(k == nk - 1)
    def _():
        o_ref[...] = x_ref[...] + gate_ref[...] * o_ref[...]


def _mlp2(a, w, x, mod, gate_idx, rows, tm, tn, tk, sides=()):
    m, d = x.shape
    f = a.shape[1]
    tn = _tile(d, tn)
    tk = _tile(f, tk)
    nk = f // tk
    return _call_with_side_casts(
        functools.partial(_mlp2_kernel, nk=nk), (a, w, x, mod), sides,
        out_shape=jax.ShapeDtypeStruct((m, d), _F32),
        grid=(m // tm, d // tn, nk),
        in_specs=[
            pl.BlockSpec((tm, tk), lambda i, j, k: (i, k)),
            pl.BlockSpec((tk, tn), lambda i, j, k: (k, j)),
            pl.BlockSpec((tm, tn), lambda i, j, k: (i, j)),
            _mod_spec(tn, gate_idx, rows, tm),
        ],
        out_specs=pl.BlockSpec((tm, tn), lambda i, j, k: (i, j)),
        name="mlp2",
        semantics=("parallel", "parallel", "arbitrary"),
        vmem_limit_bytes=V7X_VMEM_LIMIT_BIG_TILES_BYTES,
    )


def _mixers(u_pc, y_attn, p, batch, seq, tm):
    pool_width = p["pool_w"].shape[0] * p["pool_w"].shape[1]
    conv_width = p["conv_dw_w"].shape[1]
    y_pool = _pool(u_pc, p["pool_w"], p["pool_scale"], batch, seq)
    hconv = _dwconv(u_pc, pool_width, pool_width + conv_width, p["conv_dw_w"], p["conv_dw_b"], batch, seq)
    y_conv = _pw(hconv, p["conv_norm_g"], p["conv_pw_w"], tm)
    return y_pool, y_conv, y_attn


class _Bf16Weights:
    def __init__(self, raw, depth):
        self.raw, self.depth, self.done = raw, depth, {}

    def sides(self, *wanted):
        self.pending = [(n, l) for n, l in wanted if l < self.depth and (n, l) not in self.done]
        return [_SideCast(self.raw[n], l) for n, l in self.pending]

    def store(self, cast):
        self.done.update(zip(self.pending, cast))

    def __getitem__(self, key):
        return self.done[key]


def _post_attention(x, ys, p, bf, layer, mod, rows, tm, cast_ahead, ada_next=None):
    ahead = (lambda *wanted: bf.sides(*wanted)) if cast_ahead else (lambda *wanted: [])
    x, cast = _outproj(*ys, bf["w_out", layer], x, mod, 2, rows, tm, OUTPROJ_COLS, sides=ahead(("w_mlp1", layer)))
    if cast_ahead:
        bf.store(cast)
    h = _normmod(x, p["norm2_g"], mod, 3, rows)
    a, cast = _mlp1(h, bf["w_mlp1", layer], tm, MLP1_COLS,
                    sides=ahead(("w_mlp2", layer), ("w_in", layer + 1)) + ([ada_next] if ada_next else []))
    mod_next = cast.pop() if ada_next else None
    if cast_ahead:
        bf.store(cast)
    x, cast = _mlp2(a, bf["w_mlp2", layer], x, mod, 5, rows, tm, MLP2_COLS, MLP2_DEPTH,
                    sides=ahead(("w_mlp1", layer + 1), ("w_out", layer + 1)))
    if cast_ahead:
        bf.store(cast)
    return x, mod_next


def kernel(x, c, ctx, c_ctx, w_ada, b_ada, norm1_g, norm2_g, w_in, pool_w, pool_scale, conv_dw_w, conv_dw_b,
           conv_norm_g, conv_pw_w, q_norm_g, k_norm_g, rpb, w_out, w_mlp1, w_mlp2):
    batch, seq, d = x.shape
    n_ctx = ctx.shape[1]
    depth = w_in.shape[0]
    n_heads = rpb.shape[1]
    attn_w = n_heads * HEAD_DIM
    pool_width = pool_w.shape[1] * pool_w.shape[2]
    conv_width = conv_dw_w.shape[2]
    off_q = pool_width + 2 * conv_width
    off_k = off_q + attn_w
    kh = min(WIN_H, seq // GRID_W)

    m_lat, m_ctx = batch * seq, batch * n_ctx
    tm_lat, tm_ctx = _tile(seq, MATMUL_ROWS), _tile(m_ctx, MATMUL_ROWS)
    ctx_row = batch
    lat_row = _Rows(base=0, tokens=seq)
    cx_row = _Rows(base=ctx_row, tokens=m_ctx)

    x = x.reshape(m_lat, d)
    xc = ctx.reshape(m_ctx, d)
    n_rows = -(-(batch + 1) // V7X_SUBLANES) * V7X_SUBLANES
    cvec = jnp.zeros((n_rows, d), _F32).at[:batch].set(c).at[ctx_row].set(c_ctx)
    qk_gain = jnp.concatenate([jnp.tile(q_norm_g, (1, n_heads)), jnp.tile(k_norm_g, (1, n_heads))], axis=1)

    bf = _Bf16Weights({"w_in": w_in, "w_out": w_out, "w_mlp1": w_mlp1, "w_mlp2": w_mlp2}, depth)
    bf.done["w_in", 0] = _to_bf16(w_in, 0)
    for layer in range(depth):
        w_in_bf = bf["w_in", layer]
        p = {
            "pool_w": _to_bf16(pool_w.reshape(depth, -1, pool_w.shape[-1]), layer).reshape(pool_w.shape[1:]),
            "pool_scale": pool_scale[layer],
            "conv_dw_w": conv_dw_w[layer], "conv_dw_b": conv_dw_b[layer],
            "conv_norm_g": conv_norm_g[layer], "conv_pw_w": _to_bf16(conv_pw_w, layer),
            "norm2_g": norm2_g[layer],
        }
        gains = qk_gain[layer:layer + 1]
        update_ctx = layer < depth - 1
        mod = (_ada(cvec, w_ada, b_ada, 0) if layer == 0 else mod_next).reshape(n_rows, N_MOD, 1, d)
        ada_next = _SideAda(cvec, w_ada, b_ada.reshape(depth, 1, -1), layer + 1) if layer + 1 < depth else None
        bias = _bias_table(rpb[layer])

        hc = _normmod(xc, norm1_g[layer], mod, 0, cx_row)
        if update_ctx:
            uc_pc, _ = _proj(hc, w_in_bf, 0, off_q, None, _F32, tm_ctx, PROJ_COLS, "proj_pc")
            qkv_c, _ = _proj(hc, w_in_bf, off_q, 3 * attn_w, gains, _BF16, tm_ctx, PROJ_COLS, "proj_qkv")
            kc_col, vc_col = attn_w, 2 * attn_w
        else:
            qkv_c, _ = _proj(hc, w_in_bf, off_k, 2 * attn_w, gains[:, attn_w:], _BF16, tm_ctx, PROJ_COLS, "proj_kv")
            kc_col, vc_col = 0, attn_w

        h = _normmod(x, norm1_g[layer], mod, 0, lat_row)
        u_pc, _ = _proj(h, w_in_bf, 0, off_q, None, _F32, tm_lat, PROJ_COLS, "proj_pc")
        qkv, cast = _proj(h, w_in_bf, off_q, 3 * attn_w, gains, _BF16, tm_lat, PROJ_COLS, "proj_qkv",
                          sides=bf.sides(("w_out", layer)))
        bf.store(cast)
        y_attn = _nattn(qkv, qkv_c, kc_col, vc_col, bias, batch, seq, n_ctx, n_heads)
        x, mod_next = _post_attention(x, _mixers(u_pc, y_attn, p, batch, seq, tm_lat), p, bf, layer, mod, lat_row,
                                      tm_lat, cast_ahead=True, ada_next=ada_next)

        if update_ctx:
            yc_attn = _cattn(qkv_c, batch, n_ctx, n_heads)
            xc, _ = _post_attention(xc, _mixers(uc_pc, yc_attn, p, batch, n_ctx, tm_ctx), p, bf, layer, mod, cx_row,
                                    tm_ctx, cast_ahead=False)

    return x.reshape(batch, seq, d)
```

```python
import functools
import math
from typing import Callable, NamedTuple

import jax
import jax.numpy as jnp
from jax import lax
from jax.experimental import pallas as pl
from jax.experimental.pallas import tpu as pltpu

GRID_W = 64
WIN_H = 8
WIN_W = 16
HEAD_DIM = 128
POOL_WINDOWS = (2, 4, 8, 16)
CONV_KSIZE = 31
N_MOD = 6
EPS = 1e-6
MASK_VALUE = -1e30

V7X_LANES = 128
V7X_SUBLANES = 8
V7X_MXU_COLS = 256
V7X_VMEM_LIMIT_BYTES = 56 * 1024 * 1024
V7X_VMEM_LIMIT_BIG_TILES_BYTES = 60 * 1024 * 1024
MATMUL_ROWS = 1024
PROJ_COLS = 1024
OUTPROJ_COLS = 1024
OUTPROJ_COLS_WITH_SIDE_JOB = 512
MLP1_COLS = 1024
MLP2_COLS = 1024
MLP2_DEPTH = 4096
NATTN_ROWS_PER_TILE = 2
NATTN_TILES_PER_STEP = 8
NATTN_BIAS_TILES = 2 * WIN_H - 1 + 2 * (NATTN_ROWS_PER_TILE - 1)
NATTN_HEADS_PER_STEP = 4
NORMMOD_ROWS = 512
SEQ_PAD = 16

_BF16 = jnp.bfloat16
_F32 = jnp.float32


def _params(*sem):
    return pltpu.CompilerParams(dimension_semantics=sem, vmem_limit_bytes=V7X_VMEM_LIMIT_BYTES)


def _tile(n, pref, *also):
    if n <= pref and not any(a % n for a in also):
        return n
    t = min(pref, n) - min(pref, n) % V7X_LANES
    while n % t or any(a % t for a in also):
        t -= V7X_LANES
    return t


def _dot(a, b):
    return jnp.dot(a, b, preferred_element_type=_F32)


def _cast_kernel(w_ref, o_ref):
    o_ref[...] = w_ref[...].astype(o_ref.dtype)


def _to_bf16(w, layer):
    _, r, c = w.shape
    tr, tc = _tile(r, 512), _tile(c, 2048)
    return pl.pallas_call(
        _cast_kernel,
        out_shape=jax.ShapeDtypeStruct((r, c), _BF16),
        grid=(r // tr, c // tc),
        in_specs=[pl.BlockSpec((None, tr, tc), lambda i, j: (layer, i, j))],
        out_specs=pl.BlockSpec((tr, tc), lambda i, j: (i, j)),
        compiler_params=_params("parallel", "parallel"),
        name="cast_bf16",
    )(w)


class _SideCast(NamedTuple):
    w: jax.Array
    layer: int


def _linear_step(idx, grid):
    step = idx[0]
    for i, n in zip(idx[1:], grid[1:]):
        step = step * n + i
    return step


class _SideAda(NamedTuple):
    cvec: jax.Array
    w_ada: jax.Array
    b_ada: jax.Array
    layer: int


class _SidePlan(NamedTuple):
    n_blocks: int
    args: tuple
    in_specs: tuple
    out_shape: jax.ShapeDtypeStruct
    out_spec: pl.BlockSpec
    compute: Callable


def _cast_compute(src, dst):
    dst[...] = src[...].astype(dst.dtype)


def _side_plan(side, grid):
    n_steps = math.prod(grid)
    if isinstance(side, _SideCast):
        _, r, c = side.w.shape
        bf16_rows = 2 * V7X_SUBLANES
        nb = max(b for b in range(1, n_steps + 1) if r % b == 0 and (r // b) % bf16_rows == 0)
    else:
        r, d = side.cvec.shape
        n = side.w_ada.shape[2]
        tn = min(t for t in range(V7X_MXU_COLS, n + 1, V7X_MXU_COLS) if n % t == 0 and n // t <= n_steps)
        nb = n // tn

    def blk(*idx):
        return jnp.minimum(_linear_step(idx, grid), nb - 1)

    if isinstance(side, _SideCast):
        tr = r // nb
        return _SidePlan(nb, (side.w,),
                         (pl.BlockSpec((None, tr, c), lambda *idx: (side.layer, blk(*idx), 0)),),
                         jax.ShapeDtypeStruct((r, c), _BF16),
                         pl.BlockSpec((tr, c), lambda *idx: (blk(*idx), 0)),
                         _cast_compute)
    return _SidePlan(nb, (side.cvec, side.w_ada, side.b_ada),
                     (pl.BlockSpec((r, d), lambda *idx: (0, 0)),
                      pl.BlockSpec((None, d, tn), lambda *idx: (side.layer, 0, blk(*idx))),
                      pl.BlockSpec((None, 1, tn), lambda *idx: (side.layer, 0, blk(*idx)))),
                     jax.ShapeDtypeStruct((r, n), _F32),
                     pl.BlockSpec((r, tn), lambda *idx: (0, blk(*idx))),
                     _ada_kernel)


def _call_with_side_casts(body, args, sides, *, out_shape, grid, in_specs, out_specs, name, semantics=None,
                          vmem_limit_bytes=V7X_VMEM_LIMIT_BYTES):
    plans = [_side_plan(s, grid) for s in sides]
    n_in = len(args)
    n_side_in = sum(len(p.args) for p in plans)

    def wrapped(*refs):
        side_out = refs[n_in + n_side_in + 1:]

        def side():
            step = _linear_step([pl.program_id(a) for a in range(len(grid))], grid)
            first = n_in
            for p, dst in zip(plans, side_out):
                srcs = refs[first:first + len(p.args)]
                first += len(p.args)
                if p.n_blocks == math.prod(grid):
                    p.compute(*srcs, dst)
                else:
                    pl.when(step < p.n_blocks)(functools.partial(p.compute, *srcs, dst))

        body(*refs[:n_in], refs[n_in + n_side_in], side=side)

    if semantics is None or sides:
        semantics = ("arbitrary" if sides else "parallel",) * len(grid)
    outs = pl.pallas_call(
        wrapped,
        out_shape=[out_shape] + [p.out_shape for p in plans],
        grid=grid,
        in_specs=list(in_specs) + [spec for p in plans for spec in p.in_specs],
        out_specs=[out_specs] + [p.out_spec for p in plans],
        compiler_params=pltpu.CompilerParams(dimension_semantics=semantics, vmem_limit_bytes=vmem_limit_bytes),
        name=name,
    )(*args, *[a for p in plans for a in p.args])
    return outs[0], list(outs[1:])


def _ada_kernel(c_ref, w_ref, b_ref, o_ref):
    c = c_ref[...]
    s = (c * jax.nn.sigmoid(c)).astype(_BF16)
    o_ref[...] = _dot(s, w_ref[...].astype(_BF16)) + b_ref[...]


def _ada(cvec, w_ada, b_ada, layer):
    r, d = cvec.shape
    n = w_ada.shape[2]
    tn = _tile(n, 1024)
    return pl.pallas_call(
        _ada_kernel,
        out_shape=jax.ShapeDtypeStruct((r, n), _F32),
        grid=(n // tn,),
        in_specs=[
            pl.BlockSpec((r, d), lambda j: (0, 0)),
            pl.BlockSpec((None, d, tn), lambda j: (layer, 0, j)),
            pl.BlockSpec((None, 1, tn), lambda j: (layer, 0, j)),
        ],
        out_specs=pl.BlockSpec((r, tn), lambda j: (0, j)),
        compiler_params=_params("parallel"),
        name="ada",
    )(cvec, w_ada, b_ada.reshape(b_ada.shape[0], 1, n))


class _Rows(NamedTuple):
    base: int
    tokens: int

    def of_block(self, i, tm):
        return self.base + (i * tm) // self.tokens


def _mod_spec(tn, which, rows, tm):
    return pl.BlockSpec((None, None, 1, tn), lambda i, j=0, *_: (rows.of_block(i, tm), which, 0, j))


def _normmod_kernel(x_ref, g_ref, shift_ref, scale_ref, o_ref):
    group = 2 * V7X_SUBLANES
    gain = g_ref[...]
    one_plus_scale = 1.0 + scale_ref[...]
    shift = shift_ref[...]

    def body(r, carry):
        rs = pl.ds(pl.multiple_of(r * group, group), group)
        x = x_ref[rs, :]
        y = x * lax.rsqrt(jnp.mean(x * x, axis=-1, keepdims=True) + EPS) * gain
        o_ref[rs, :] = (y * one_plus_scale + shift).astype(o_ref.dtype)
        return carry

    lax.fori_loop(0, x_ref.shape[0] // group, body, 0, unroll=4)


def _normmod(x, g, mod, shift_idx, rows):
    m, d = x.shape
    tm = _tile(rows.tokens, NORMMOD_ROWS)
    return pl.pallas_call(
        _normmod_kernel,
        out_shape=jax.ShapeDtypeStruct((m, d), _BF16),
        grid=(m // tm,),
        in_specs=[
            pl.BlockSpec((tm, d), lambda i: (i, 0)),
            pl.BlockSpec((1, d), lambda i: (0, 0)),
            _mod_spec(d, shift_idx, rows, tm),
            _mod_spec(d, shift_idx + 1, rows, tm),
        ],
        out_specs=pl.BlockSpec((tm, d), lambda i: (i, 0)),
        compiler_params=_params("parallel"),
        name="normmod",
    )(x, g.reshape(1, d), mod, mod)


def _col_chunks(n):
    c = V7X_MXU_COLS if n % V7X_MXU_COLS == 0 else n
    return [slice(s, s + c) for s in range(0, n, c)]


def _proj_plain_kernel(h_ref, w_ref, o_ref, *, side):
    h = h_ref[...]
    for sl in _col_chunks(o_ref.shape[1]):
        o_ref[:, sl] = _dot(h, w_ref[:, sl]).astype(o_ref.dtype)
    side()


def _proj_kernel(h_ref, w_ref, g_ref, o_ref, *, n_norm_blocks, side):
    normed = pl.program_id(1) < n_norm_blocks
    h = h_ref[...]
    for sl in _col_chunks(w_ref.shape[1]):
        acc = _dot(h, w_ref[:, sl])
        for c in range(0, acc.shape[1], HEAD_DIM):
            a = acc[:, c:c + HEAD_DIM]
            rstd = lax.rsqrt(jnp.mean(a * a, axis=-1, keepdims=True) + EPS)
            gain = g_ref[:, sl.start + c:sl.start + c + HEAD_DIM]
            y = a * jnp.where(normed, rstd, 1.0) * jnp.where(normed, gain, 1.0)
            o_ref[(sl.start + c) // HEAD_DIM] = y.astype(o_ref.dtype)
    side()


def _proj(h, w, col0, ncols, gains, out_dtype, tm, tn, name, sides=()):
    m, d = h.shape
    tn = _tile(ncols, tn, col0, *(() if gains is None else (gains.shape[1],)))
    cb = col0 // tn
    in_specs = [
        pl.BlockSpec((tm, d), lambda i, j: (i, 0)),
        pl.BlockSpec((d, tn), lambda i, j: (0, cb + j)),
    ]
    if gains is None:
        body, args = _proj_plain_kernel, (h, w)
        out_shape = jax.ShapeDtypeStruct((m, ncols), out_dtype)
        out_spec = pl.BlockSpec((tm, tn), lambda i, j: (i, j))
    else:
        n_norm_blocks = gains.shape[1] // tn
        body, args = functools.partial(_proj_kernel, n_norm_blocks=n_norm_blocks), (h, w, gains)
        in_specs.append(pl.BlockSpec((1, tn), lambda i, j: (0, jnp.minimum(j, n_norm_blocks - 1))))
        out_shape = jax.ShapeDtypeStruct((ncols // HEAD_DIM, m, HEAD_DIM), out_dtype)
        out_spec = pl.BlockSpec((tn // HEAD_DIM, tm, HEAD_DIM), lambda i, j: (j, i, 0))
    return _call_with_side_casts(
        body, args, sides,
        out_shape=out_shape,
        grid=(m // tm, ncols // tn),
        in_specs=in_specs,
        out_specs=out_spec,
        name=name,
    )


def _nattn_kernel(q_ref, k_ref, v_ref, kc_ref, vc_ref, b_ref, o_ref, v1_ref, vc1_ref, *, rows, kh):
    for g in range(q_ref.shape[0]):
        _nattn_head(q_ref.at[g], k_ref.at[g], v_ref.at[g], kc_ref.at[g], vc_ref.at[g], b_ref.at[g],
                    o_ref.at[:, g * HEAD_DIM:(g + 1) * HEAD_DIM], v1_ref.at[g], vc1_ref.at[g], rows=rows, kh=kh)


def _nattn_head(q_ref, k_ref, v_ref, kc_ref, vc_ref, b_ref, o_ref, v1_ref, vc1_ref, *, rows, kh):
    exp2_scale = HEAD_DIM ** -0.5 * math.log2(math.e)
    nq = NATTN_ROWS_PER_TILE
    nu = min(kh + nq - 1, rows)
    nt = (((1,), (1,)), ((), ()))
    tn = (((0,), (0,)), ((), ()))
    kc = kc_ref[...]
    lane_row = lax.broadcasted_iota(jnp.int32, (1, nq * GRID_W), 1) // GRID_W
    for src, dst in ((v_ref, v1_ref), (vc_ref, vc1_ref)):
        dst[:, :HEAD_DIM] = src[...]
        dst[:, HEAD_DIM:] = jnp.ones(src.shape, dst.dtype)
    vc1 = vc1_ref[...]

    def scores(p):
        i = nq * p
        ustart = jnp.clip(i - kh // 2, 0, rows - nu)
        lo = jnp.clip(i + lane_row - kh // 2, 0, rows - kh) - ustart
        e0 = ustart - i + (WIN_H + nq - 2)
        qsl = pl.ds(pl.multiple_of(i * GRID_W, nq * GRID_W), nq * GRID_W)
        win = pl.ds(pl.multiple_of(ustart * GRID_W, GRID_W), nu * GRID_W)
        qs = q_ref[qsl, :]
        bias = jnp.concatenate(
            [jnp.where((lo <= r) & (r < lo + kh), b_ref[e0 + r], MASK_VALUE) for r in range(nu)], axis=0)
        t_nb = lax.dot_general(k_ref[win, :], qs, nt, preferred_element_type=_F32) + bias
        t_cx = lax.dot_general(kc, qs, nt, preferred_element_type=_F32)
        return qsl, win, t_nb, t_cx

    def weights(t_nb, t_cx):
        m = jnp.maximum(jnp.max(t_nb, axis=0, keepdims=True), jnp.max(t_cx, axis=0, keepdims=True))
        return (jnp.exp2((t_nb - m) * exp2_scale).astype(_BF16), jnp.exp2((t_cx - m) * exp2_scale).astype(_BF16))

    def step(t, carry):
        staged = [scores(NATTN_TILES_PER_STEP * t + g) for g in range(NATTN_TILES_PER_STEP)]
        probs = [weights(t_nb, t_cx) for _, _, t_nb, t_cx in staged]
        for (qsl, win, _, _), (e_nb, e_cx) in zip(staged, probs):
            r = lax.dot_general(e_nb, v1_ref[win, :], tn, preferred_element_type=_F32)
            r = r + lax.dot_general(e_cx, vc1, tn, preferred_element_type=_F32)
            o_ref[qsl, :] = (r[:, :HEAD_DIM] / r[:, HEAD_DIM:]).astype(o_ref.dtype)
        return carry

    lax.fori_loop(0, rows // (nq * NATTN_TILES_PER_STEP), step, 0)


def _bias_table(rpb):
    nq = NATTN_ROWS_PER_TILE
    j = jnp.arange(GRID_W)
    col_start = jnp.clip(j - WIN_W // 2, 0, GRID_W - WIN_W)
    valid = (j[None, :] >= col_start[:, None]) & (j[None, :] < col_start[:, None] + WIN_W)
    dc = jnp.clip(j[None, :] - j[:, None] + (WIN_W - 1), 0, 2 * WIN_W - 2)
    onehot = (dc[None] == jnp.arange(2 * WIN_W - 1)[:, None, None]).astype(_F32)
    t = jnp.einsum("hdc,cqk->hdqk", rpb, onehot, precision=lax.Precision.HIGHEST)
    t = jnp.where(valid[None, None], t * HEAD_DIM ** 0.5, MASK_VALUE)
    t = jnp.swapaxes(t, 2, 3)
    masked = jnp.full_like(t[:, :1], MASK_VALUE)
    padded = jnp.concatenate([masked] * (2 * (nq - 1)) + [t] + [masked] * (nq - 1), axis=1)
    parts = [padded[:, nq - 1 - qr:nq - 1 - qr + NATTN_BIAS_TILES] for qr in range(nq)]
    return jnp.concatenate(parts, axis=-1).astype(_F32)


def _nattn(qkv, kvc, kc_col, vc_col, bias, batch, seq, n_ctx, n_heads):
    rows = seq // GRID_W
    kh = min(WIN_H, rows)
    g = NATTN_HEADS_PER_STEP
    assert rows % (NATTN_ROWS_PER_TILE * NATTN_TILES_PER_STEP) == 0 and n_heads % g == 0
    a = n_heads * HEAD_DIM
    hb = lambda off: off // (g * HEAD_DIM)
    lat = lambda off: pl.BlockSpec((g, seq, HEAD_DIM), lambda b, h: (hb(off) + h, b, 0))
    cxt = lambda off: pl.BlockSpec((g, n_ctx, HEAD_DIM), lambda b, h: (hb(off) + h, b, 0))
    return pl.pallas_call(
        functools.partial(_nattn_kernel, rows=rows, kh=kh),
        out_shape=jax.ShapeDtypeStruct((batch * seq, a), _BF16),
        grid=(batch, n_heads // g),
        in_specs=[
            lat(0), lat(a), lat(2 * a), cxt(kc_col), cxt(vc_col),
            pl.BlockSpec((g, NATTN_BIAS_TILES, GRID_W, NATTN_ROWS_PER_TILE * GRID_W), lambda b, h: (h, 0, 0, 0)),
        ],
        out_specs=pl.BlockSpec((seq, g * HEAD_DIM), lambda b, h: (b, h)),
        scratch_shapes=[pltpu.VMEM((g, seq, 2 * HEAD_DIM), _BF16), pltpu.VMEM((g, n_ctx, 2 * HEAD_DIM), _BF16)],
        compiler_params=_params("parallel", "parallel"),
        name="nattn",
    )(qkv, qkv, qkv, kvc, kvc, bias)


def _cattn_kernel(q_ref, k_ref, v_ref, o_ref):
    scale = HEAD_DIM ** -0.5
    s = lax.dot_general(q_ref[...], k_ref[...], (((1,), (1,)), ((), ())), preferred_element_type=_F32) * scale
    e = jnp.exp(s - jnp.max(s, axis=-1, keepdims=True))
    p = e * (1.0 / jnp.sum(e, axis=-1, keepdims=True))
    o_ref[...] = _dot(p.astype(_BF16), v_ref[...]).astype(o_ref.dtype)


def _cattn(qkv, batch, n_ctx, n_heads):
    a = n_heads * HEAD_DIM
    spec = lambda off: pl.BlockSpec((None, n_ctx, HEAD_DIM), lambda b, h: (off // HEAD_DIM + h, b, 0))
    return pl.pallas_call(
        _cattn_kernel,
        out_shape=jax.ShapeDtypeStruct((batch * n_ctx, a), _BF16),
        grid=(batch, n_heads),
        in_specs=[spec(0), spec(a), spec(2 * a)],
        out_specs=pl.BlockSpec((n_ctx, HEAD_DIM), lambda b, h: (b, h)),
        compiler_params=_params("parallel", "parallel"),
        name="cattn",
    )(qkv, qkv, qkv)


def _fill_padded(pad_ref, seq, fill_chunk, chunk):
    zeros = jnp.zeros((SEQ_PAD, pad_ref.shape[1]), pad_ref.dtype)
    pad_ref[pl.ds(0, SEQ_PAD), :] = zeros
    pad_ref[pl.ds(SEQ_PAD + seq, SEQ_PAD), :] = zeros

    def body(c, carry):
        t0 = pl.multiple_of(c * chunk, chunk)
        pad_ref[pl.ds(SEQ_PAD + t0, chunk), :] = fill_chunk(t0)
        return carry

    lax.fori_loop(0, seq // chunk, body, 0)


def _pool_kernel(u_ref, w_ref, s_ref, o_ref, pad_ref, *, seq, chunk):
    g = pl.program_id(1)
    _fill_padded(pad_ref, seq, lambda t0: u_ref[pl.ds(t0, chunk), :], chunk)
    w = w_ref[...]
    s = s_ref[...]

    for gi, win in enumerate(POOL_WINDOWS):

        @pl.when(g == gi)
        def _(win=win):
            def body(c, carry):
                t0 = pl.multiple_of(c * chunk, chunk)
                blk = pad_ref[pl.ds(t0, chunk + 2 * SEQ_PAD), :]
                run, step = blk, 1
                while step < win:
                    run = run + pltpu.roll(run, run.shape[0] - step, axis=0)
                    step *= 2
                tot = run[SEQ_PAD - win // 2:SEQ_PAD - win // 2 + chunk, :]
                t = t0 + lax.broadcasted_iota(jnp.int32, (chunk, blk.shape[1]), 0)
                lo = jnp.maximum(t - win // 2, 0)
                hi = jnp.minimum(t + (win - 1 - win // 2), seq - 1)
                cnt = (hi - lo + 1).astype(_F32)
                pooled = tot / cnt - blk[SEQ_PAD:SEQ_PAD + chunk, :]
                o_ref[pl.ds(t0, chunk), :] = (_dot(pooled.astype(_BF16), w) * s).astype(o_ref.dtype)
                return carry

            lax.fori_loop(0, seq // chunk, body, 0)


def _pool(u, pool_w, pool_scale, batch, seq):
    n_groups, c, _ = pool_w.shape
    chunk = min(128, seq)
    return pl.pallas_call(
        functools.partial(_pool_kernel, seq=seq, chunk=chunk),
        out_shape=jax.ShapeDtypeStruct((batch * seq, n_groups * c), _BF16),
        grid=(batch, n_groups),
        in_specs=[
            pl.BlockSpec((seq, c), lambda b, g: (b, g)),
            pl.BlockSpec((None, c, c), lambda b, g: (g, 0, 0)),
            pl.BlockSpec((1, c), lambda b, g: (0, g)),
        ],
        out_specs=pl.BlockSpec((seq, c), lambda b, g: (b, g)),
        scratch_shapes=[pltpu.VMEM((seq + 2 * SEQ_PAD, c), _F32)],
        compiler_params=_params("parallel", "parallel"),
        name="pool",
    )(u, pool_w, pool_scale.reshape(1, n_groups * c))


def _dwconv_kernel(a_ref, gate_ref, w_ref, b_ref, o_ref, pad_ref, wb_ref, *, seq, chunk):
    def glu(t0):
        return a_ref[pl.ds(t0, chunk), :] * jax.nn.sigmoid(gate_ref[pl.ds(t0, chunk), :])

    _fill_padded(pad_ref, seq, glu, chunk)
    n_ch = o_ref.shape[1]
    for k in range(CONV_KSIZE):
        wb_ref[pl.ds(V7X_SUBLANES * k, V7X_SUBLANES), :] = jnp.broadcast_to(w_ref[k:k + 1, :], (V7X_SUBLANES, n_ch))
    wb_ref[pl.ds(V7X_SUBLANES * CONV_KSIZE, V7X_SUBLANES), :] = jnp.broadcast_to(b_ref[...], (V7X_SUBLANES, n_ch))
    half = CONV_KSIZE // 2
    span = chunk + 2 * SEQ_PAD - V7X_SUBLANES
    groups = chunk // V7X_SUBLANES

    def tap(k):
        return wb_ref[pl.ds(V7X_SUBLANES * k, V7X_SUBLANES), :][None]

    def body(c, carry):
        t0 = pl.multiple_of(c * chunk, chunk)
        blk = pad_ref[pl.ds(t0, chunk + 2 * SEQ_PAD), :]
        acc = jnp.broadcast_to(tap(CONV_KSIZE), (groups, V7X_SUBLANES, n_ch))
        for res in range(V7X_SUBLANES):
            shifted = None
            for k in range(CONV_KSIZE):
                off = SEQ_PAD - half + k
                if off % V7X_SUBLANES != res:
                    continue
                if shifted is None:
                    shifted = blk[res:res + span, :]
                base = off - res
                acc = acc + shifted[base:base + chunk, :].reshape(groups, V7X_SUBLANES, n_ch) * tap(k)
        o_ref[pl.ds(t0, chunk), :] = acc.reshape(chunk, n_ch)
        return carry

    lax.fori_loop(0, seq // chunk, body, 0)


def _dwconv(u, a_col, gate_col, dw_w, dw_b, batch, seq):
    ksz, c = dw_w.shape
    tc = V7X_LANES
    chunk = min(128, seq)
    return pl.pallas_call(
        functools.partial(_dwconv_kernel, seq=seq, chunk=chunk),
        out_shape=jax.ShapeDtypeStruct((batch * seq, c), _F32),
        grid=(batch, c // tc),
        in_specs=[
            pl.BlockSpec((seq, tc), lambda b, j: (b, a_col // tc + j)),
            pl.BlockSpec((seq, tc), lambda b, j: (b, gate_col // tc + j)),
            pl.BlockSpec((ksz, tc), lambda b, j: (0, j)),
            pl.BlockSpec((1, tc), lambda b, j: (0, j)),
        ],
        out_specs=pl.BlockSpec((seq, tc), lambda b, j: (b, j)),
        scratch_shapes=[pltpu.VMEM((seq + 2 * SEQ_PAD, tc), _F32),
                        pltpu.VMEM((V7X_SUBLANES * (ksz + 1), tc), _F32)],
        compiler_params=_params("parallel", "parallel"),
        name="dwconv",
    )(u, u, dw_w, dw_b.reshape(1, c))


def _pw_kernel(h_ref, g_ref, w_ref, o_ref):
    x = h_ref[...]
    y = x * lax.rsqrt(jnp.mean(x * x, axis=-1, keepdims=True) + EPS) * g_ref[...]
    y = y * jax.nn.sigmoid(y)
    o_ref[...] = _dot(y.astype(_BF16), w_ref[...]).astype(o_ref.dtype)


def _pw(h, g, w, tm):
    m, c = h.shape
    return pl.pallas_call(
        _pw_kernel,
        out_shape=jax.ShapeDtypeStruct((m, c), _BF16),
        grid=(m // tm,),
        in_specs=[
            pl.BlockSpec((tm, c), lambda i: (i, 0)),
            pl.BlockSpec((1, c), lambda i: (0, 0)),
            pl.BlockSpec((c, c), lambda i: (0, 0)),
        ],
        out_specs=pl.BlockSpec((tm, c), lambda i: (i, 0)),
        compiler_params=_params("parallel"),
        name="conv_pw",
    )(h, g.reshape(1, c), w)


def _outproj_kernel(yp_ref, yc_ref, ya_ref, w_ref, x_ref, gate_ref, o_ref, *, side):
    kp = yp_ref.shape[1]
    kc = yc_ref.shape[1]
    for sl in _col_chunks(o_ref.shape[1]):
        acc = _dot(yp_ref[...], w_ref[pl.ds(0, kp), sl])
        acc = acc + _dot(yc_ref[...], w_ref[pl.ds(kp, kc), sl])
        acc = acc + _dot(ya_ref[...], w_ref[pl.ds(kp + kc, ya_ref.shape[1]), sl])
        o_ref[:, sl] = x_ref[:, sl] + gate_ref[:, sl] * acc
    side()


def _outproj(yp, yc, ya, w, x, mod, gate_idx, rows, tm, tn, sides=()):
    m, d = x.shape
    tn = _tile(d, tn)
    lhs = lambda y: pl.BlockSpec((tm, y.shape[1]), lambda i, j: (i, 0))
    return _call_with_side_casts(
        _outproj_kernel, (yp, yc, ya, w, x, mod), sides,
        out_shape=jax.ShapeDtypeStruct((m, d), _F32),
        grid=(m // tm, d // tn),
        in_specs=[
            lhs(yp), lhs(yc), lhs(ya),
            pl.BlockSpec((w.shape[0], tn), lambda i, j: (0, j)),
            pl.BlockSpec((tm, tn), lambda i, j: (i, j)),
            _mod_spec(tn, gate_idx, rows, tm),
        ],
        out_specs=pl.BlockSpec((tm, tn), lambda i, j: (i, j)),
        name="outproj",
    )


def _mlp1_kernel(h_ref, w_ref, o_ref, *, side):
    for sl in _col_chunks(o_ref.shape[1]):
        a = jnp.maximum(_dot(h_ref[...], w_ref[:, sl]), 0.0)
        o_ref[:, sl] = (a * a).astype(o_ref.dtype)
    side()


def _mlp1(h, w, tm, tn, sides=()):
    m, d = h.shape
    f = w.shape[1]
    tn = _tile(f, tn)
    return _call_with_side_casts(
        _mlp1_kernel, (h, w), sides,
        out_shape=jax.ShapeDtypeStruct((m, f), _BF16),
        grid=(m // tm, f // tn),
        in_specs=[
            pl.BlockSpec((tm, d), lambda i, j: (i, 0)),
            pl.BlockSpec((d, tn), lambda i, j: (0, j)),
        ],
        out_specs=pl.BlockSpec((tm, tn), lambda i, j: (i, j)),
        name="mlp1",
    )


def _mlp2_kernel(a_ref, w_ref, x_ref, gate_ref, o_ref, *, nk, side):
    k = pl.program_id(2)
    chunks = _col_chunks(o_ref.shape[1])

    @pl.when(k == 0)
    def _():
        for sl in chunks:
            o_ref[:, sl] = _dot(a_ref[...], w_ref[:, sl])
        side()

    @pl.when(k > 0)
    def _():
        for sl in chunks:
            o_ref[:, sl] += _dot(a_ref[...], w_ref[:, sl])
        side()

    @pl.when(k == nk - 1)
    def _():
        o_ref[...] = x_ref[...] + gate_ref[...] * o_ref[...]


def _mlp2(a, w, x, mod, gate_idx, rows, tm, tn, tk, sides=()):
    m, d = x.shape
    f = a.shape[1]
    tn = _tile(d, tn)
    tk = _tile(f, tk)
    nk = f // tk
    return _call_with_side_casts(
        functools.partial(_mlp2_kernel, nk=nk), (a, w, x, mod), sides,
        out_shape=jax.ShapeDtypeStruct((m, d), _F32),
        grid=(m // tm, d // tn, nk),
        in_specs=[
            pl.BlockSpec((tm, tk), lambda i, j, k: (i, k)),
            pl.BlockSpec((tk, tn), lambda i, j, k: (k, j)),
            pl.BlockSpec((tm, tn), lambda i, j, k: (i, j)),
            _mod_spec(tn, gate_idx, rows, tm),
        ],
        out_specs=pl.BlockSpec((tm, tn), lambda i, j, k: (i, j)),
        name="mlp2",
        semantics=("parallel", "parallel", "arbitrary"),
        vmem_limit_bytes=V7X_VMEM_LIMIT_BIG_TILES_BYTES,
    )


def _mixers(u_pc, y_attn, p, batch, seq, tm):
    pool_width = p["pool_w"].shape[0] * p["pool_w"].shape[1]
    conv_width = p["conv_dw_w"].shape[1]
    y_pool = _pool(u_pc, p["pool_w"], p["pool_scale"], batch, seq)
    hconv = _dwconv(u_pc, pool_width, pool_width + conv_width, p["conv_dw_w"], p["conv_dw_b"], batch, seq)
    y_conv = _pw(hconv, p["conv_norm_g"], p["conv_pw_w"], tm)
    return y_pool, y_conv, y_attn


class _Bf16Weights:
    def __init__(self, raw, depth):
        self.raw, self.depth, self.done = raw, depth, {}

    def sides(self, *wanted):
        self.pending = [(n, l) for n, l in wanted if l < self.depth and (n, l) not in self.done]
        return [_SideCast(self.raw[n], l) for n, l in self.pending]

    def store(self, cast):
        self.done.update(zip(self.pending, cast))

    def __getitem__(self, key):
        return self.done[key]


def _post_attention(x, ys, p, bf, layer, mod, rows, tm, cast_ahead, ada_next=None):
    ahead = (lambda *wanted: bf.sides(*wanted)) if cast_ahead else (lambda *wanted: [])
    sides = ahead(("w_mlp1", layer))
    x, cast = _outproj(*ys, bf["w_out", layer], x, mod, 2, rows, tm,
                       OUTPROJ_COLS_WITH_SIDE_JOB if sides else OUTPROJ_COLS, sides=sides)
    if cast_ahead:
        bf.store(cast)
    h = _normmod(x, p["norm2_g"], mod, 3, rows)
    a, cast = _mlp1(h, bf["w_mlp1", layer], tm, MLP1_COLS,
                    sides=ahead(("w_mlp2", layer), ("w_in", layer + 1)) + ([ada_next] if ada_next else []))
    mod_next = cast.pop() if ada_next else None
    if cast_ahead:
        bf.store(cast)
    x, cast = _mlp2(a, bf["w_mlp2", layer], x, mod, 5, rows, tm, MLP2_COLS, MLP2_DEPTH,
                    sides=ahead(("w_mlp1", layer + 1), ("w_out", layer + 1)))
    if cast_ahead:
        bf.store(cast)
    return x, mod_next


def kernel(x, c, ctx, c_ctx, w_ada, b_ada, norm1_g, norm2_g, w_in, pool_w, pool_scale, conv_dw_w, conv_dw_b,
           conv_norm_g, conv_pw_w, q_norm_g, k_norm_g, rpb, w_out, w_mlp1, w_mlp2):
    batch, seq, d = x.shape
    n_ctx = ctx.shape[1]
    depth = w_in.shape[0]
    n_heads = rpb.shape[1]
    attn_w = n_heads * HEAD_DIM
    pool_width = pool_w.shape[1] * pool_w.shape[2]
    conv_width = conv_dw_w.shape[2]
    off_q = pool_width + 2 * conv_width
    off_k = off_q + attn_w
    kh = min(WIN_H, seq // GRID_W)

    m_lat, m_ctx = batch * seq, batch * n_ctx
    tm_lat, tm_ctx = _tile(seq, MATMUL_ROWS), _tile(m_ctx, MATMUL_ROWS)
    ctx_row = batch
    lat_row = _Rows(base=0, tokens=seq)
    cx_row = _Rows(base=ctx_row, tokens=m_ctx)

    x = x.reshape(m_lat, d)
    xc = ctx.reshape(m_ctx, d)
    n_rows = -(-(batch + 1) // V7X_SUBLANES) * V7X_SUBLANES
    cvec = jnp.zeros((n_rows, d), _F32).at[:batch].set(c).at[ctx_row].set(c_ctx)
    qk_gain = jnp.concatenate([jnp.tile(q_norm_g, (1, n_heads)), jnp.tile(k_norm_g, (1, n_heads))], axis=1)

    bf = _Bf16Weights({"w_in": w_in, "w_out": w_out, "w_mlp1": w_mlp1, "w_mlp2": w_mlp2}, depth)
    bf.done["w_in", 0] = _to_bf16(w_in, 0)
    for layer in range(depth):
        w_in_bf = bf["w_in", layer]
        p = {
            "pool_w": _to_bf16(pool_w.reshape(depth, -1, pool_w.shape[-1]), layer).reshape(pool_w.shape[1:]),
            "pool_scale": pool_scale[layer],
            "conv_dw_w": conv_dw_w[layer], "conv_dw_b": conv_dw_b[layer],
            "conv_norm_g": conv_norm_g[layer], "conv_pw_w": _to_bf16(conv_pw_w, layer),
            "norm2_g": norm2_g[layer],
        }
        gains = qk_gain[layer:layer + 1]
        update_ctx = layer < depth - 1
        mod = (_ada(cvec, w_ada, b_ada, 0) if layer == 0 else mod_next).reshape(n_rows, N_MOD, 1, d)
        ada_next = _SideAda(cvec, w_ada, b_ada.reshape(depth, 1, -1), layer + 1) if layer + 1 < depth else None
        bias = _bias_table(rpb[layer])

        hc = _normmod(xc, norm1_g[layer], mod, 0, cx_row)
        if update_ctx:
            uc_pc, _ = _proj(hc, w_in_bf, 0, off_q, None, _F32, tm_ctx, PROJ_COLS, "proj_pc")
            qkv_c, _ = _proj(hc, w_in_bf, off_q, 3 * attn_w, gains, _BF16, tm_ctx, PROJ_COLS, "proj_qkv")
            kc_col, vc_col = attn_w, 2 * attn_w
        else:
            qkv_c, _ = _proj(hc, w_in_bf, off_k, 2 * attn_w, gains[:, attn_w:], _BF16, tm_ctx, PROJ_COLS, "proj_kv")
            kc_col, vc_col = 0, attn_w

        h = _normmod(x, norm1_g[layer], mod, 0, lat_row)
        u_pc, _ = _proj(h, w_in_bf, 0, off_q, None, _F32, tm_lat, PROJ_COLS, "proj_pc")
        qkv, cast = _proj(h, w_in_bf, off_q, 3 * attn_w, gains, _BF16, tm_lat, PROJ_COLS, "proj_qkv",
                          sides=bf.sides(("w_out", layer)))
        bf.store(cast)
        y_attn = _nattn(qkv, qkv_c, kc_col, vc_col, bias, batch, seq, n_ctx, n_heads)
        x, mod_next = _post_attention(x, _mixers(u_pc, y_attn, p, batch, seq, tm_lat), p, bf, layer, mod, lat_row,
                                      tm_lat, cast_ahead=True, ada_next=ada_next)

        if update_ctx:
            yc_attn = _cattn(qkv_c, batch, n_ctx, n_heads)
            xc, _ = _post_attention(xc, _mixers(uc_pc, yc_attn, p, batch, n_ctx, tm_ctx), p, bf, layer, mod, cx_row,
                                    tm_ctx, cast_ahead=False)

    return x.reshape(batch, seq, d)
```
